```python
import jax, jax.numpy as jnp
from jax import lax
import numpy as np

D_MODEL = 1024
BATCH = 8
SEQ = 2048
DEPTH = 1
DEC_BATCH = 128
DEC_SEQ = 8
PAST_LEN = 16384
PAGE_SIZE = 128

POOL_WINDOWS = (2, 4, 8, 16)
N_POOL_GROUPS = 4
D_POOL_GROUP = D_MODEL // 8
D_POOL = N_POOL_GROUPS * D_POOL_GROUP
POOL_PAD = max(POOL_WINDOWS) - 1
CHUNK = 128
N_SGU_HEADS = 4
D_SGU = D_MODEL // 2
D_SGU_HEAD = D_SGU // N_SGU_HEADS
D_IN = D_POOL + 2 * D_SGU + 2 * D_MODEL
D_FF = ((8 * D_MODEL // 3 + 255) // 256) * 256
CONV_K = 3
EPS = 1e-6

kernel_name = "gated_pool_sgu_convffn_step"


def _rmsnorm(x, g):
    xf = x.astype(jnp.float32)
    r = lax.rsqrt(jnp.mean(xf * xf, axis=-1, keepdims=True) + EPS)
    return (xf * r).astype(x.dtype) * g


def _pool_branch(p_ext, T, full_windows, pool_w, pool_scale):
    B = p_ext.shape[0]
    cs = jnp.cumsum(p_ext.astype(jnp.float32), axis=1)
    cs = jnp.pad(cs, ((0, 0), (1, 0), (0, 0)))
    x_tok = p_ext[:, POOL_PAD:].astype(jnp.float32)
    t = jnp.arange(T, dtype=jnp.float32)[:, None]
    outs = []
    for g, w in enumerate(POOL_WINDOWS):
        lo, hi = g * D_POOL_GROUP, (g + 1) * D_POOL_GROUP
        s = cs[:, POOL_PAD + 1:POOL_PAD + 1 + T, lo:hi] - cs[:, POOL_PAD + 1 - w:POOL_PAD + 1 - w + T, lo:hi]
        cnt = jnp.full_like(t, float(w)) if full_windows else jnp.minimum(float(w), t + 1.0)
        outs.append(s / cnt - x_tok[..., lo:hi])
    d = jnp.stack(outs, axis=2).astype(p_ext.dtype)
    y = jnp.einsum('btgc,gcd->btgd', d, pool_w).reshape(B, T, D_POOL)
    return y * pool_scale


def _sgu_branch(z, n_chunks, L, sgu_norm_g, sgu_w, sgu_b):
    B, T, _ = z.shape
    u, v = z[..., :D_SGU], z[..., D_SGU:]
    v = _rmsnorm(v, sgu_norm_g)
    vh = v.reshape(B, n_chunks, L, N_SGU_HEADS, D_SGU_HEAD)
    mask = jnp.tril(jnp.ones((L, L), dtype=bool))
    w_s = jnp.where(mask[None], sgu_w[:, :L, :L], jnp.zeros((), sgu_w.dtype))
    mix = jnp.einsum('hts,bcshd->bcthd', w_s, vh) + jnp.transpose(sgu_b[:, :L])[None, None, :, :, None]
    out = u * mix.reshape(B, T, D_SGU)
    return out, v


def _layer(x, pool_prev, conv_prev, full_windows, n_chunks, L,
           norm1_g, w_in, pool_w, pool_scale, w_pool_out, sgu_norm_g, sgu_w, sgu_b,
           w_sgu_out, w_o, norm2_g, ffn_w_up, ffn_w_gate, ffn_conv_w, ffn_conv_b, ffn_w_down):
    B, T, _ = x.shape
    h = _rmsnorm(x, norm1_g)
    zin = h @ w_in
    p_in = zin[..., :D_POOL]
    z_sgu = zin[..., D_POOL:D_POOL + 2 * D_SGU]
    g_in = zin[..., D_POOL + 2 * D_SGU:]
    p_ext = jnp.concatenate([pool_prev.astype(p_in.dtype), p_in], axis=1)
    a_out = _pool_branch(p_ext, T, full_windows, pool_w, pool_scale) @ w_pool_out
    s_out, v_rows = _sgu_branch(jax.nn.gelu(z_sgu), n_chunks, L, sgu_norm_g, sgu_w, sgu_b)
    b_out = s_out @ w_sgu_out
    gates = jax.nn.sigmoid(g_in.astype(jnp.float32)).astype(x.dtype)
    ga, gb = gates[..., :D_MODEL], gates[..., D_MODEL:]
    x = x + (ga * a_out + gb * b_out) @ w_o
    h2 = _rmsnorm(x, norm2_g)
    a = h2 @ ffn_w_up
    ext = jnp.concatenate([conv_prev.astype(a.dtype), a], axis=1)
    c = ext[:, 0:T] * ffn_conv_w[0]
    for k in range(1, CONV_K):
        c = c + ext[:, k:k + T] * ffn_conv_w[k]
    f = jax.nn.gelu(c + ffn_conv_b) * (h2 @ ffn_w_gate)
    x = x + f @ ffn_w_down
    return x, p_ext[:, -POOL_PAD:], ext[:, -(CONV_K - 1):], v_rows


def setup_inputs(seed: int = 0) -> dict:
    key = jax.random.key(seed)
    ks = jax.random.split(key, 24)
    f32 = jnp.float32
    n = lambda k, s, sc: jax.random.normal(k, s, f32) * sc
    return {
        "x_prompt": n(ks[0], (BATCH, SEQ, D_MODEL), 1.0),
        "x_sample": n(ks[1], (DEC_BATCH, DEC_SEQ, D_MODEL), 1.0),
        "state_pool": n(ks[2], (DEPTH, DEC_BATCH, POOL_PAD, D_POOL), 1.0),
        "state_ffn_conv": n(ks[3], (DEPTH, DEC_BATCH, CONV_K - 1, D_FF), 1.0),
        "norm1_g": 1.0 + n(ks[4], (DEPTH, D_MODEL), 0.02),
        "w_in": n(ks[5], (DEPTH, D_MODEL, D_IN), D_MODEL ** -0.5),
        "pool_w": n(ks[6], (DEPTH, N_POOL_GROUPS, D_POOL_GROUP, D_POOL_GROUP), D_POOL_GROUP ** -0.5),
        "pool_scale": 1.0 + n(ks[7], (DEPTH, D_POOL), 0.02),
        "w_pool_out": n(ks[8], (DEPTH, D_POOL, D_MODEL), D_POOL ** -0.5),
        "sgu_norm_g": 1.0 + n(ks[9], (DEPTH, D_SGU), 0.02),
        "sgu_w": n(ks[10], (DEPTH, N_SGU_HEADS, CHUNK, CHUNK), CHUNK ** -0.5),
        "sgu_b": 1.0 + n(ks[11], (DEPTH, N_SGU_HEADS, CHUNK), 0.1),
        "w_sgu_out": n(ks[12], (DEPTH, D_SGU, D_MODEL), D_SGU ** -0.5),
        "w_o": n(ks[13], (DEPTH, D_MODEL, D_MODEL), D_MODEL ** -0.5),
        "norm2_g": 1.0 + n(ks[14], (DEPTH, D_MODEL), 0.02),
        "ffn_w_up": n(ks[15], (DEPTH, D_MODEL, D_FF), D_MODEL ** -0.5),
        "ffn_w_gate": n(ks[16], (DEPTH, D_MODEL, D_FF), D_MODEL ** -0.5),
        "ffn_conv_w": n(ks[17], (DEPTH, CONV_K, D_FF), CONV_K ** -0.5),
        "ffn_conv_b": n(ks[18], (DEPTH, D_FF), 0.02),
        "ffn_w_down": n(ks[19], (DEPTH, D_FF, D_MODEL), D_FF ** -0.5),
        "final_norm_g": 1.0 + n(ks[20], (D_MODEL,), 0.02),
    }


def reference(x_prompt, x_sample, state_pool, state_ffn_conv, norm1_g, w_in, pool_w, pool_scale,
              w_pool_out, sgu_norm_g, sgu_w, sgu_b, w_sgu_out, w_o, norm2_g, ffn_w_up, ffn_w_gate,
              ffn_conv_w, ffn_conv_b, ffn_w_down, final_norm_g):
    B, T = x_prompt.shape[0], x_prompt.shape[1]
    DB, TS = x_sample.shape[0], x_sample.shape[1]
    xp, xs = x_prompt, x_sample
    pool_p, pool_s, conv_p, conv_s, v_s = [], [], [], [], []
    for l in range(DEPTH):
        w = (norm1_g[l], w_in[l], pool_w[l], pool_scale[l], w_pool_out[l], sgu_norm_g[l], sgu_w[l],
             sgu_b[l], w_sgu_out[l], w_o[l], norm2_g[l], ffn_w_up[l], ffn_w_gate[l], ffn_conv_w[l],
             ffn_conv_b[l], ffn_w_down[l])
        zp_pool = jnp.zeros((B, POOL_PAD, D_POOL), xp.dtype)
        zp_conv = jnp.zeros((B, CONV_K - 1, D_FF), xp.dtype)
        xp, npool_p, nconv_p, _ = _layer(xp, zp_pool, zp_conv, False, T // CHUNK, CHUNK, *w)
        xs, npool_s, nconv_s, v_rows = _layer(xs, state_pool[l], state_ffn_conv[l], True, 1, TS, *w)
        pool_p.append(npool_p); pool_s.append(npool_s)
        conv_p.append(nconv_p); conv_s.append(nconv_s); v_s.append(v_rows)
    y_prompt = _rmsnorm(xp, final_norm_g)
    y_sample = _rmsnorm(xs, final_norm_g)
    new_pool_prompt = jnp.stack(pool_p, axis=0)
    new_pool_sample = jnp.stack(pool_s, axis=0)
    new_conv_prompt = jnp.stack(conv_p, axis=0)
    new_conv_sample = jnp.stack(conv_s, axis=0)
    new_sgu_v_sample = jnp.stack(v_s, axis=0)
    return (y_prompt, y_sample, new_pool_prompt, new_pool_sample, new_conv_prompt, new_conv_sample, new_sgu_v_sample)
```

```python
import functools

import numpy as np
import jax
import jax.numpy as jnp
from jax import lax
from jax.experimental import pallas as pl
from jax.experimental.pallas import tpu as pltpu

D_MODEL = 1024
POOL_WINDOWS = (2, 4, 8, 16)
N_GROUPS = 4
D_GROUP = 128
D_POOL = N_GROUPS * D_GROUP
POOL_PAD = max(POOL_WINDOWS) - 1
POOL_HIST = POOL_PAD + 1
CHUNK = 128
N_HEADS = 4
D_SGU = 512
D_HEAD = D_SGU // N_HEADS
D_IN = D_POOL + 2 * D_SGU + 2 * D_MODEL
D_FF = 2816
CONV_K = 3
EPS = 1e-6

SUBLANES = 8
FF_CHUNK = 256
PROMPT_TILE = 256
SAMPLE_TILE = 256
VMEM_LIMIT_BYTES = 56 * 1024 * 1024

_GELU_C = 0.7978845608028654
_GELU_C3 = _GELU_C * 0.044715

BF16 = jnp.bfloat16
F32 = jnp.float32


def _dot(a, b):
    return jnp.dot(a, b, preferred_element_type=F32)


def _rms(x, g):
    ms = jnp.mean(x * x, axis=-1, keepdims=True)
    return (x * lax.rsqrt(ms + EPS)) * g


def _gelu(x):
    u = x * (_GELU_C + _GELU_C3 * (x * x))
    return x * (0.5 + 0.5 * jnp.tanh(u))


def _sigmoid(x):
    return 0.5 * jnp.tanh(0.5 * x) + 0.5


def _split_bf16(x):
    hi = x.astype(BF16)
    lo = (x - hi.astype(F32)).astype(BF16)
    return hi, lo


def _mixer_merge(x, h, a_out, mix, u, w_in_ref, wso_ref, wo_ref):
    s_out = (u * mix).astype(BF16)
    b_out = _dot(s_out, wso_ref[...])
    g_lo = D_POOL + 2 * D_SGU
    ga = _sigmoid(_dot(h, w_in_ref[:, g_lo:g_lo + D_MODEL]))
    gb = _sigmoid(_dot(h, w_in_ref[:, g_lo + D_MODEL:g_lo + 2 * D_MODEL]))
    m = (ga * a_out + gb * b_out).astype(BF16)
    return x + _dot(m, wo_ref[...])


def _pool_project(d, pw_ref, pscale_ref, wpo_ref):
    y01 = _dot(d[:, 0:2 * D_GROUP], pw_ref[0])
    y23 = _dot(d[:, 2 * D_GROUP:], pw_ref[1])
    y = jnp.concatenate([y01, y23], axis=1) * pscale_ref[...]
    return _dot(y.astype(BF16), wpo_ref[...])


def _sgu_inputs(h, w_in_ref, sgug_ref):
    u = _gelu(_dot(h, w_in_ref[:, D_POOL:D_POOL + D_SGU]))
    v = _gelu(_dot(h, w_in_ref[:, D_POOL + D_SGU:D_POOL + 2 * D_SGU]))
    return u, _rms(v, sgug_ref[...])


def _prompt_kernel(x_ref, g1_ref, w_in_ref, pw_ref, pscale_ref, wpo_ref, sgug_ref, sguw_ref,
                   sgub_ref, wso_ref, wo_ref, g2_ref, wup_ref, wgate_ref, cw_ref, cb_ref,
                   wdown_ref, gf_ref, invw_ref, tbl_ref,
                   y_ref, npool_ref, nconv_ref, carry_p, carry_a):
    tt = PROMPT_TILE
    t_idx = pl.program_id(1)

    @pl.when(t_idx == 0)
    def _():
        carry_p[...] = jnp.zeros_like(carry_p)
        carry_a[...] = jnp.zeros_like(carry_a)

    x = x_ref[0]
    h = _rms(x, g1_ref[...]).astype(BF16)

    p = _dot(h, w_in_ref[:, 0:D_POOL])
    ext = jnp.concatenate([carry_p[...], p], axis=0)
    sums = []
    level = ext
    for g, w in enumerate(POOL_WINDOWS):
        level = level + pltpu.roll(level, w // 2, axis=0)
        sums.append(level[:, 0:D_GROUP])
        if g + 1 < N_GROUPS:
            level = level[:, D_GROUP:]
    win = jnp.concatenate(sums, axis=1)[POOL_HIST:]
    first = jnp.where(t_idx == 0, tbl_ref[...], jnp.broadcast_to(invw_ref[...], tbl_ref.shape))
    scaled = jnp.concatenate([win[:POOL_HIST] * first, win[POOL_HIST:] * invw_ref[...]], axis=0)
    d = (scaled - p).astype(BF16)
    carry_p[...] = p[tt - POOL_HIST:]
    npool_ref[0] = p[tt - POOL_HIST:]
    a_out = _pool_project(d, pw_ref, pscale_ref, wpo_ref)

    u, v = _sgu_inputs(h, w_in_ref, sgug_ref)
    vb = v.astype(BF16)
    n_chunks = tt // CHUNK
    row = lax.broadcasted_iota(jnp.int32, (CHUNK, CHUNK), 0)
    col = lax.broadcasted_iota(jnp.int32, (CHUNK, CHUNK), 1)
    per_head = []
    for hh in range(N_HEADS):
        w_s = jnp.where(row >= col, sguw_ref[hh], 0.0).astype(BF16)
        rhs = jnp.concatenate(
            [vb[c * CHUNK:(c + 1) * CHUNK, hh * D_HEAD:(hh + 1) * D_HEAD] for c in range(n_chunks)],
            axis=1)
        res = _dot(w_s, rhs)
        bias = sgub_ref[:, hh * D_HEAD:(hh + 1) * D_HEAD]
        per_head.append([res[:, c * D_HEAD:(c + 1) * D_HEAD] + bias for c in range(n_chunks)])
    mix = jnp.concatenate(
        [jnp.concatenate([per_head[hh][c] for hh in range(N_HEADS)], axis=1) for c in range(n_chunks)],
        axis=0)

    x1 = _mixer_merge(x, h, a_out, mix, u, w_in_ref, wso_ref, wo_ref)

    h2 = _rms(x1, g2_ref[...]).astype(BF16)
    acc = None
    for j in range(D_FF // FF_CHUNK):
        lo, hi = j * FF_CHUNK, (j + 1) * FF_CHUNK
        a = _dot(h2, wup_ref[:, lo:hi])
        ext_a = jnp.concatenate([carry_a[:, lo:hi], a], axis=0)
        s1 = pltpu.roll(ext_a, 1, axis=0)[SUBLANES:]
        s2 = pltpu.roll(ext_a, 2, axis=0)[SUBLANES:]
        c = s2 * cw_ref[0:1, lo:hi] + s1 * cw_ref[1:2, lo:hi] + a * cw_ref[2:3, lo:hi]
        carry_a[:, lo:hi] = a[tt - SUBLANES:]
        nconv_ref[0, :, lo:hi] = a[tt - SUBLANES:]
        f = (_gelu(c + cb_ref[:, lo:hi]) * _dot(h2, wgate_ref[:, lo:hi])).astype(BF16)
        part = _dot(f, wdown_ref[lo:hi, :])
        acc = part if acc is None else acc + part
    x2 = x1 + acc
    y_ref[0] = _rms(x2, gf_ref[...])


def _sample_kernel(x_ref, pstate_ref, hist_ref, g1_ref, w_in_ref, pw_ref, pscale_ref, wpo_ref,
                   sgug_ref, sguwt_ref, sgub_ref, wso_ref, wo_ref, g2_ref, wup_ref, wgate_ref,
                   cw_ref, cb_ref, wdown_ref, gf_ref, pool_a_ref, pool_b_ref,
                   y_ref, p_ref, a_ref, v_ref, *, dec_seq):
    rt = SAMPLE_TILE
    x = x_ref[...]
    h = _rms(x, g1_ref[...]).astype(BF16)

    p = _dot(h, w_in_ref[:, 0:D_POOL])
    p_ref[...] = p
    p_hi, p_lo = _split_bf16(p)
    s_hi, s_lo = _split_bf16(pstate_ref[...])
    means = []
    for g in range(N_GROUPS):
        sl = slice(g * D_GROUP, (g + 1) * D_GROUP)
        new2 = jnp.concatenate([p_hi[:, sl], p_lo[:, sl]], axis=1)
        old2 = jnp.concatenate([s_hi[:, sl], s_lo[:, sl]], axis=1)
        r = _dot(pool_a_ref[g], new2) + _dot(pool_b_ref[g], old2)
        means.append(r[:, :D_GROUP] + r[:, D_GROUP:])
    d = (jnp.concatenate(means, axis=1) - p).astype(BF16)
    a_out = _pool_project(d, pw_ref, pscale_ref, wpo_ref)

    u, v = _sgu_inputs(h, w_in_ref, sgug_ref)
    v_ref[...] = v
    vb = v.astype(BF16)
    row = lax.broadcasted_iota(jnp.int32, (rt, rt), 0)
    col = lax.broadcasted_iota(jnp.int32, (rt, rt), 1)
    keep = ((row // dec_seq) == (col // dec_seq)) & (row >= col)
    mixes = []
    for hh in range(N_HEADS):
        w_s = jnp.where(keep, sguwt_ref[hh], 0.0).astype(BF16)
        mixes.append(_dot(w_s, vb[:, hh * D_HEAD:(hh + 1) * D_HEAD]))
    mix = jnp.concatenate(mixes, axis=1) + sgub_ref[...]

    x1 = _mixer_merge(x, h, a_out, mix, u, w_in_ref, wso_ref, wo_ref)

    h2 = _rms(x1, g2_ref[...]).astype(BF16)
    tok = lax.broadcasted_iota(jnp.int32, (rt, FF_CHUNK), 0) % dec_seq
    acc = None
    for j in range(D_FF // FF_CHUNK):
        lo, hi = j * FF_CHUNK, (j + 1) * FF_CHUNK
        a = _dot(h2, wup_ref[:, lo:hi])
        a_ref[:, lo:hi] = a
        hist = hist_ref[:, lo:hi]
        s1 = jnp.where(tok < 1, pltpu.roll(hist, rt - (dec_seq - 1), axis=0), pltpu.roll(a, 1, axis=0))
        s2 = jnp.where(tok < 2, pltpu.roll(hist, rt - (dec_seq - 2), axis=0), pltpu.roll(a, 2, axis=0))
        c = s2 * cw_ref[0:1, lo:hi] + s1 * cw_ref[1:2, lo:hi] + a * cw_ref[2:3, lo:hi]
        f = (_gelu(c + cb_ref[:, lo:hi]) * _dot(h2, wgate_ref[:, lo:hi])).astype(BF16)
        part = _dot(f, wdown_ref[lo:hi, :])
        acc = part if acc is None else acc + part
    x2 = x1 + acc
    y_ref[...] = _rms(x2, gf_ref[...])


def _resident(shape):
    zeros = (0,) * len(shape)
    return pl.BlockSpec(shape, lambda *_: zeros, pipeline_mode=pl.Buffered(1))


def _pool_matrices(dec_seq, n_batch):
    a = np.zeros((N_GROUPS, dec_seq, dec_seq), np.float32)
    b = np.zeros((N_GROUPS, dec_seq, POOL_HIST), np.float32)
    for g, w in enumerate(POOL_WINDOWS):
        for t in range(dec_seq):
            for k in range(w):
                i = POOL_PAD + t - k
                if i >= POOL_PAD:
                    a[g, t, i - POOL_PAD] = 1.0 / w
                else:
                    b[g, t, i + 1] = 1.0 / w
    eye = np.eye(n_batch, dtype=np.float32)
    a_bd = np.stack([np.kron(eye, a[g]) for g in range(N_GROUPS)])
    b_bd = np.stack([np.kron(eye, b[g]) for g in range(N_GROUPS)])
    return jnp.asarray(a_bd, BF16), jnp.asarray(b_bd, BF16)


def _first_rows_table():
    t = np.arange(POOL_HIST, dtype=np.float32)[:, None]
    w = np.repeat(np.asarray(POOL_WINDOWS, np.float32), D_GROUP)[None, :]
    return jnp.asarray(1.0 / np.minimum(w, t + 1.0), F32), jnp.asarray(1.0 / w, F32)


def kernel(x_prompt, x_sample, state_pool, state_ffn_conv, norm1_g, w_in, pool_w, pool_scale,
           w_pool_out, sgu_norm_g, sgu_w, sgu_b, w_sgu_out, w_o, norm2_g, ffn_w_up, ffn_w_gate,
           ffn_conv_w, ffn_conv_b, ffn_w_down, final_norm_g):
    depth = norm1_g.shape[0]
    assert depth == 1
    batch, seq, _ = x_prompt.shape
    dec_batch, dec_seq, _ = x_sample.shape
    assert seq % PROMPT_TILE == 0 and PROMPT_TILE % CHUNK == 0 and seq >= POOL_HIST
    assert SAMPLE_TILE % dec_seq == 0 and (dec_batch * dec_seq) % SAMPLE_TILE == 0
    assert CONV_K - 1 <= dec_seq <= CHUNK

    l = 0
    row = lambda v: v.reshape(1, -1)
    zeros_g = jnp.zeros((D_GROUP, D_GROUP), F32)
    pw = jnp.stack([
        jnp.block([[pool_w[l, 0], zeros_g], [zeros_g, pool_w[l, 1]]]),
        jnp.block([[pool_w[l, 2], zeros_g], [zeros_g, pool_w[l, 3]]]),
    ]).astype(BF16)
    shared = dict(
        g1=row(norm1_g[l]), w_in=w_in[l].astype(BF16), pw=pw, pscale=row(pool_scale[l]),
        wpo=w_pool_out[l].astype(BF16), sgug=row(sgu_norm_g[l]), wso=w_sgu_out[l].astype(BF16),
        wo=w_o[l].astype(BF16), g2=row(norm2_g[l]), wup=ffn_w_up[l].astype(BF16),
        wgate=ffn_w_gate[l].astype(BF16), cw=ffn_conv_w[l], cb=row(ffn_conv_b[l]),
        wdown=ffn_w_down[l].astype(BF16), gf=row(final_norm_g))

    tbl, invw = _first_rows_table()
    sgub_p = jnp.repeat(jnp.transpose(sgu_b[l, :, :CHUNK]), D_HEAD, axis=1)
    p_inputs = [x_prompt, shared["g1"], shared["w_in"], shared["pw"], shared["pscale"], shared["wpo"],
                shared["sgug"], sgu_w[l, :, :CHUNK, :CHUNK], sgub_p, shared["wso"], shared["wo"],
                shared["g2"], shared["wup"], shared["wgate"], shared["cw"], shared["cb"],
                shared["wdown"], shared["gf"], invw, tbl]
    p_specs = [pl.BlockSpec((1, PROMPT_TILE, D_MODEL), lambda b, t: (b, t, 0))]
    p_specs += [_resident(a.shape) for a in p_inputs[1:]]
    y_prompt, npool_p, nconv_p = pl.pallas_call(
        _prompt_kernel,
        grid=(batch, seq // PROMPT_TILE),
        in_specs=p_specs,
        out_specs=[pl.BlockSpec((1, PROMPT_TILE, D_MODEL), lambda b, t: (b, t, 0)),
                   pl.BlockSpec((1, POOL_HIST, D_POOL), lambda b, t: (b, 0, 0)),
                   pl.BlockSpec((1, SUBLANES, D_FF), lambda b, t: (b, 0, 0))],
        out_shape=[jax.ShapeDtypeStruct((batch, seq, D_MODEL), F32),
                   jax.ShapeDtypeStruct((batch, POOL_HIST, D_POOL), F32),
                   jax.ShapeDtypeStruct((batch, SUBLANES, D_FF), F32)],
        scratch_shapes=[pltpu.VMEM((POOL_HIST, D_POOL), F32), pltpu.VMEM((SUBLANES, D_FF), F32)],
        compiler_params=pltpu.CompilerParams(dimension_semantics=("arbitrary", "arbitrary"),
                                             vmem_limit_bytes=VMEM_LIMIT_BYTES),
        name="prompt_layer",
    )(*p_inputs)
    new_pool_prompt = npool_p[None, :, POOL_HIST - POOL_PAD:]
    new_conv_prompt = nconv_p[None, :, SUBLANES - (CONV_K - 1):]

    rows = dec_batch * dec_seq
    tile_batch = SAMPLE_TILE // dec_seq
    pool_a, pool_b = _pool_matrices(dec_seq, tile_batch)
    xs = x_sample.reshape(rows, D_MODEL)
    pstate = jnp.pad(state_pool[l], ((0, 0), (1, 0), (0, 0))).reshape(dec_batch * POOL_HIST, D_POOL)
    hist = jnp.pad(state_ffn_conv[l], ((0, 0), (dec_seq - (CONV_K - 1), 0), (0, 0))).reshape(rows, D_FF)
    sguw_t = jnp.tile(sgu_w[l, :, :dec_seq, :dec_seq], (1, tile_batch, tile_batch))
    sgub_s = jnp.tile(jnp.repeat(jnp.transpose(sgu_b[l, :, :dec_seq]), D_HEAD, axis=1), (tile_batch, 1))
    s_inputs = [xs, pstate, hist, shared["g1"], shared["w_in"], shared["pw"], shared["pscale"],
                shared["wpo"], shared["sgug"], sguw_t, sgub_s, shared["wso"], shared["wo"], shared["g2"],
                shared["wup"], shared["wgate"], shared["cw"], shared["cb"], shared["wdown"], shared["gf"],
                pool_a, pool_b]
    s_specs = [pl.BlockSpec((SAMPLE_TILE, D_MODEL), lambda i: (i, 0)),
               pl.BlockSpec((tile_batch * POOL_HIST, D_POOL), lambda i: (i, 0)),
               pl.BlockSpec((SAMPLE_TILE, D_FF), lambda i: (i, 0))]
    s_specs += [_resident(a.shape) for a in s_inputs[3:]]
    tiled = lambda n: pl.BlockSpec((SAMPLE_TILE, n), lambda i: (i, 0))
    y_s, p_s, a_s, v_s = pl.pallas_call(
        functools.partial(_sample_kernel, dec_seq=dec_seq),
        grid=(rows // SAMPLE_TILE,),
        in_specs=s_specs,
        out_specs=[tiled(D_MODEL), tiled(D_POOL), tiled(D_FF), tiled(D_SGU)],
        out_shape=[jax.ShapeDtypeStruct((rows, D_MODEL), F32),
                   jax.ShapeDtypeStruct((rows, D_POOL), F32),
                   jax.ShapeDtypeStruct((rows, D_FF), F32),
                   jax.ShapeDtypeStruct((rows, D_SGU), F32)],
        compiler_params=pltpu.CompilerParams(dimension_semantics=("arbitrary",),
                                             vmem_limit_bytes=VMEM_LIMIT_BYTES),
        name="sample_layer",
    )(*s_inputs)
    y_sample = y_s.reshape(dec_batch, dec_seq, D_MODEL)
    p_new = p_s.reshape(dec_batch, dec_seq, D_POOL)
    new_pool_sample = jnp.concatenate([state_pool[l], p_new], axis=1)[None, :, -POOL_PAD:]
    new_conv_sample = a_s.reshape(dec_batch, dec_seq, D_FF)[None, :, -(CONV_K - 1):]
    new_sgu_v_sample = v_s.reshape(1, dec_batch, dec_seq, D_SGU)
    return (y_prompt, y_sample, new_pool_prompt, new_pool_sample, new_conv_prompt, new_conv_sample,
            new_sgu_v_sample)
```

```python
import functools

import numpy as np
import jax
import jax.numpy as jnp
from jax import lax
from jax.experimental import pallas as pl
from jax.experimental.pallas import tpu as pltpu

D_MODEL = 1024
POOL_WINDOWS = (2, 4, 8, 16)
N_GROUPS = 4
D_GROUP = 128
D_POOL = N_GROUPS * D_GROUP
POOL_PAD = max(POOL_WINDOWS) - 1
POOL_HIST = POOL_PAD + 1
CHUNK = 128
N_HEADS = 4
D_SGU = 512
D_HEAD = D_SGU // N_HEADS
D_IN = D_POOL + 2 * D_SGU + 2 * D_MODEL
D_FF = 2816
CONV_K = 3
EPS = 1e-6

SUBLANES = 8
FF_CHUNK = 256
PROMPT_TILE = 256
SAMPLE_TILE = 256
VMEM_LIMIT_BYTES = 56 * 1024 * 1024

_GELU_C = 0.7978845608028654
_GELU_C3 = _GELU_C * 0.044715

BF16 = jnp.bfloat16
F32 = jnp.float32


def _dot(a, b):
    return jnp.dot(a, b, preferred_element_type=F32)


def _rms(x, g):
    ms = jnp.mean(x * x, axis=-1, keepdims=True)
    return (x * lax.rsqrt(ms + EPS)) * g


def _gelu(x):
    u = x * (_GELU_C + _GELU_C3 * (x * x))
    return x * (0.5 + 0.5 * jnp.tanh(u))


def _sigmoid(x):
    return 0.5 * jnp.tanh(0.5 * x) + 0.5


def _split_bf16(x):
    hi = x.astype(BF16)
    lo = (x - hi.astype(F32)).astype(BF16)
    return hi, lo


def _input_proj(h, w_in_ref, sgug_ref):
    p = _dot(h, w_in_ref[:, 0:D_POOL])
    u = _gelu(_dot(h, w_in_ref[:, D_POOL:D_POOL + D_SGU]))
    v = _rms(_gelu(_dot(h, w_in_ref[:, D_POOL + D_SGU:D_POOL + 2 * D_SGU])), sgug_ref[...])
    g_lo = D_POOL + 2 * D_SGU
    ga = _sigmoid(_dot(h, w_in_ref[:, g_lo:g_lo + D_MODEL]))
    gb = _sigmoid(_dot(h, w_in_ref[:, g_lo + D_MODEL:g_lo + 2 * D_MODEL]))
    return p, u, v, ga, gb


def _pool_groups(d, pw_ref, pscale_ref):
    y01 = _dot(d[:, 0:2 * D_GROUP], pw_ref[0])
    y23 = _dot(d[:, 2 * D_GROUP:], pw_ref[1])
    return (jnp.concatenate([y01, y23], axis=1) * pscale_ref[...]).astype(BF16)


def _mixer_merge(x, ga, gb, y, mix, u, wpo_ref, wso_ref, wo_ref):
    a_out = _dot(y, wpo_ref[...])
    b_out = _dot((u * mix).astype(BF16), wso_ref[...])
    m = (ga * a_out + gb * b_out).astype(BF16)
    return x + _dot(m, wo_ref[...])


def _prompt_kernel(x_ref, g1_ref, w_in_ref, pw_ref, pscale_ref, wpo_ref, sgug_ref, sguw_ref,
                   sgub_ref, wso_ref, wo_ref, g2_ref, wup_ref, wgate_ref, cw_ref, cb_ref,
                   wdown_ref, gf_ref, invw_ref, tbl_ref,
                   y_ref, npool_ref, nconv_ref, carry_p, carry_a):
    tt = PROMPT_TILE
    t_idx = pl.program_id(1)

    @pl.when(t_idx == 0)
    def _():
        carry_p[...] = jnp.zeros_like(carry_p)
        carry_a[...] = jnp.zeros_like(carry_a)

    x = x_ref[0]
    h = _rms(x, g1_ref[...]).astype(BF16)

    p, u, v, ga, gb = _input_proj(h, w_in_ref, sgug_ref)

    ext = jnp.concatenate([carry_p[...], p], axis=0)
    sums = []
    level = ext
    for g, w in enumerate(POOL_WINDOWS):
        level = level + pltpu.roll(level, w // 2, axis=0)
        sums.append(level[:, 0:D_GROUP])
        if g + 1 < N_GROUPS:
            level = level[:, D_GROUP:]
    win = jnp.concatenate(sums, axis=1)[POOL_HIST:]
    first = jnp.where(t_idx == 0, tbl_ref[...], jnp.broadcast_to(invw_ref[...], tbl_ref.shape))
    scaled = jnp.concatenate([win[:POOL_HIST] * first, win[POOL_HIST:] * invw_ref[...]], axis=0)
    d = (scaled - p).astype(BF16)
    carry_p[...] = p[tt - POOL_HIST:]
    npool_ref[0] = p[tt - POOL_HIST:]
    y = _pool_groups(d, pw_ref, pscale_ref)

    vb = v.astype(BF16)
    n_chunks = tt // CHUNK
    row = lax.broadcasted_iota(jnp.int32, (CHUNK, CHUNK), 0)
    col = lax.broadcasted_iota(jnp.int32, (CHUNK, CHUNK), 1)
    per_head = []
    for hh in range(N_HEADS):
        w_s = jnp.where(row >= col, sguw_ref[hh], 0.0).astype(BF16)
        rhs = jnp.concatenate(
            [vb[c * CHUNK:(c + 1) * CHUNK, hh * D_HEAD:(hh + 1) * D_HEAD] for c in range(n_chunks)],
            axis=1)
        res = _dot(w_s, rhs)
        bias = sgub_ref[:, hh * D_HEAD:(hh + 1) * D_HEAD]
        per_head.append([res[:, c * D_HEAD:(c + 1) * D_HEAD] + bias for c in range(n_chunks)])
    mix = jnp.concatenate(
        [jnp.concatenate([per_head[hh][c] for hh in range(N_HEADS)], axis=1) for c in range(n_chunks)],
        axis=0)

    x1 = _mixer_merge(x, ga, gb, y, mix, u, wpo_ref, wso_ref, wo_ref)

    h2 = _rms(x1, g2_ref[...]).astype(BF16)
    f_parts = []
    for j in range(D_FF // FF_CHUNK):
        lo, hi = j * FF_CHUNK, (j + 1) * FF_CHUNK
        a = _dot(h2, wup_ref[:, lo:hi])
        ext_a = jnp.concatenate([carry_a[:, lo:hi], a], axis=0)
        s1 = pltpu.roll(ext_a, 1, axis=0)[SUBLANES:]
        s2 = pltpu.roll(ext_a, 2, axis=0)[SUBLANES:]
        c = s2 * cw_ref[0:1, lo:hi] + s1 * cw_ref[1:2, lo:hi] + a * cw_ref[2:3, lo:hi]
        carry_a[:, lo:hi] = a[tt - SUBLANES:]
        nconv_ref[0, :, lo:hi] = a[tt - SUBLANES:]
        f_parts.append((_gelu(c + cb_ref[:, lo:hi]) * _dot(h2, wgate_ref[:, lo:hi])).astype(BF16))
    x2 = x1 + _dot(jnp.concatenate(f_parts, axis=1), wdown_ref[...])
    y_ref[0] = _rms(x2, gf_ref[...])


def _sample_kernel(x_ref, pstate_ref, hist_ref, g1_ref, w_in_ref, pw_ref, pscale_ref, wpo_ref,
                   sgug_ref, sguwt_ref, sgub_ref, wso_ref, wo_ref, g2_ref, wup_ref, wgate_ref,
                   cw_ref, cb_ref, wdown_ref, gf_ref, pool_a_ref, pool_b_ref,
                   y_ref, p_ref, a_ref, v_ref, *, dec_seq):
    rt = SAMPLE_TILE
    x = x_ref[...]
    h = _rms(x, g1_ref[...]).astype(BF16)

    p, u, v, ga, gb = _input_proj(h, w_in_ref, sgug_ref)
    p_ref[...] = p
    v_ref[...] = v

    p_hi, p_lo = _split_bf16(p)
    s_hi, s_lo = _split_bf16(pstate_ref[...])
    means = []
    for g in range(N_GROUPS):
        sl = slice(g * D_GROUP, (g + 1) * D_GROUP)
        new2 = jnp.concatenate([p_hi[:, sl], p_lo[:, sl]], axis=1)
        old2 = jnp.concatenate([s_hi[:, sl], s_lo[:, sl]], axis=1)
        r = _dot(pool_a_ref[g], new2) + _dot(pool_b_ref[g], old2)
        means.append(r[:, :D_GROUP] + r[:, D_GROUP:])
    d = (jnp.concatenate(means, axis=1) - p).astype(BF16)
    y = _pool_groups(d, pw_ref, pscale_ref)

    vb = v.astype(BF16)
    row = lax.broadcasted_iota(jnp.int32, (rt, rt), 0)
    col = lax.broadcasted_iota(jnp.int32, (rt, rt), 1)
    keep = ((row // dec_seq) == (col // dec_seq)) & (row >= col)
    mixes = []
    for hh in range(N_HEADS):
        w_s = jnp.where(keep, sguwt_ref[hh], 0.0).astype(BF16)
        mixes.append(_dot(w_s, vb[:, hh * D_HEAD:(hh + 1) * D_HEAD]))
    mix = jnp.concatenate(mixes, axis=1) + sgub_ref[...]

    x1 = _mixer_merge(x, ga, gb, y, mix, u, wpo_ref, wso_ref, wo_ref)

    h2 = _rms(x1, g2_ref[...]).astype(BF16)
    tok =lax.broadcasted_iota(jnp.int32, (rt, FF_CHUNK), 0) % dec_seq
    f_parts = []
    for j in range(D_FF // FF_CHUNK):
        lo, hi = j * FF_CHUNK, (j + 1) * FF_CHUNK
        a = _dot(h2, wup_ref[:, lo:hi])
        a_ref[:, lo:hi] = a
        hist = hist_ref[:, lo:hi]
        s1 = jnp.where(tok < 1, pltpu.roll(hist, rt - (dec_seq - 1), axis=0), pltpu.roll(a, 1, axis=0))
        s2 = jnp.where(tok < 2, pltpu.roll(hist, rt - (dec_seq - 2), axis=0), pltpu.roll(a, 2, axis=0))
        c = s2 * cw_ref[0:1, lo:hi] + s1 * cw_ref[1:2, lo:hi] + a * cw_ref[2:3, lo:hi]
        f_parts.append((_gelu(c + cb_ref[:, lo:hi]) * _dot(h2, wgate_ref[:, lo:hi])).astype(BF16))
    x2 = x1 + _dot(jnp.concatenate(f_parts, axis=1), wdown_ref[...])
    y_ref[...] = _rms(x2, gf_ref[...])


def _resident(shape):
    zeros = (0,) * len(shape)
    return pl.BlockSpec(shape, lambda *_: zeros, pipeline_mode=pl.Buffered(1))


def _pool_matrices(dec_seq, n_batch):
    a = np.zeros((N_GROUPS, dec_seq, dec_seq), np.float32)
    b = np.zeros((N_GROUPS, dec_seq, POOL_HIST), np.float32)
    for g, w in enumerate(POOL_WINDOWS):
        for t in range(dec_seq):
            for k in range(w):
                i = POOL_PAD + t - k
                if i >= POOL_PAD:
                    a[g, t, i - POOL_PAD] = 1.0 / w
                else:
                    b[g, t, i + 1] = 1.0 / w
    eye = np.eye(n_batch, dtype=np.float32)
    a_bd = np.stack([np.kron(eye, a[g]) for g in range(N_GROUPS)])
    b_bd = np.stack([np.kron(eye, b[g]) for g in range(N_GROUPS)])
    return jnp.asarray(a_bd, BF16), jnp.asarray(b_bd, BF16)


def _first_rows_table():
    t = np.arange(POOL_HIST, dtype=np.float32)[:, None]
    w = np.repeat(np.asarray(POOL_WINDOWS, np.float32), D_GROUP)[None, :]
    return jnp.asarray(1.0 / np.minimum(w, t + 1.0), F32), jnp.asarray(1.0 / w, F32)


def kernel(x_prompt, x_sample, state_pool, state_ffn_conv, norm1_g, w_in, pool_w, pool_scale,
           w_pool_out, sgu_norm_g, sgu_w, sgu_b, w_sgu_out, w_o, norm2_g, ffn_w_up, ffn_w_gate,
           ffn_conv_w, ffn_conv_b, ffn_w_down, final_norm_g):
    depth = norm1_g.shape[0]
    assert depth == 1
    batch, seq, _ = x_prompt.shape
    dec_batch, dec_seq, _ = x_sample.shape
    assert seq % PROMPT_TILE == 0 and PROMPT_TILE % CHUNK == 0 and seq >= POOL_HIST
    assert SAMPLE_TILE % dec_seq == 0 and (dec_batch * dec_seq) % SAMPLE_TILE == 0
    assert CONV_K - 1 <= dec_seq <= CHUNK

    l = 0
    row = lambda v: v.reshape(1, -1)
    zeros_g = jnp.zeros((D_GROUP, D_GROUP), F32)
    pw = jnp.stack([
        jnp.block([[pool_w[l, 0], zeros_g], [zeros_g, pool_w[l, 1]]]),
        jnp.block([[pool_w[l, 2], zeros_g], [zeros_g, pool_w[l, 3]]]),
    ]).astype(BF16)
    shared = dict(
        g1=row(norm1_g[l]), w_in=w_in[l].astype(BF16), pw=pw, pscale=row(pool_scale[l]),
        wpo=w_pool_out[l].astype(BF16), sgug=row(sgu_norm_g[l]), wso=w_sgu_out[l].astype(BF16),
        wo=w_o[l].astype(BF16), g2=row(norm2_g[l]), wup=ffn_w_up[l].astype(BF16),
        wgate=ffn_w_gate[l].astype(BF16), cw=ffn_conv_w[l], cb=row(ffn_conv_b[l]),
        wdown=ffn_w_down[l].astype(BF16), gf=row(final_norm_g))

    tbl, invw = _first_rows_table()
    sgub_p = jnp.repeat(jnp.transpose(sgu_b[l, :, :CHUNK]), D_HEAD, axis=1)
    p_inputs = [x_prompt, shared["g1"], shared["w_in"], shared["pw"], shared["pscale"], shared["wpo"],
                shared["sgug"], sgu_w[l, :, :CHUNK, :CHUNK], sgub_p, shared["wso"], shared["wo"],
                shared["g2"], shared["wup"], shared["wgate"], shared["cw"], shared["cb"],
                shared["wdown"], shared["gf"], invw, tbl]
    p_specs = [pl.BlockSpec((1, PROMPT_TILE, D_MODEL), lambda b, t: (b, t, 0))]
    p_specs += [_resident(a.shape) for a in p_inputs[1:]]
    y_prompt, npool_p, nconv_p = pl.pallas_call(
        _prompt_kernel,
        grid=(batch, seq // PROMPT_TILE),
        in_specs=p_specs,
        out_specs=[pl.BlockSpec((1, PROMPT_TILE, D_MODEL), lambda b, t: (b, t, 0)),
                   pl.BlockSpec((1, POOL_HIST, D_POOL), lambda b, t: (b, 0, 0)),
                   pl.BlockSpec((1, SUBLANES, D_FF), lambda b, t: (b, 0, 0))],
        out_shape=[jax.ShapeDtypeStruct((batch, seq, D_MODEL), F32),
                   jax.ShapeDtypeStruct((batch, POOL_HIST, D_POOL), F32),
                   jax.ShapeDtypeStruct((batch, SUBLANES, D_FF), F32)],
        scratch_shapes=[pltpu.VMEM((POOL_HIST, D_POOL), F32), pltpu.VMEM((SUBLANES, D_FF), F32)],
        compiler_params=pltpu.CompilerParams(dimension_semantics=("arbitrary", "arbitrary"),
                                             vmem_limit_bytes=VMEM_LIMIT_BYTES),
        name="prompt_layer",
    )(*p_inputs)
    new_pool_prompt = npool_p[None, :, POOL_HIST - POOL_PAD:]
    new_conv_prompt = nconv_p[None, :, SUBLANES - (CONV_K - 1):]

    rows = dec_batch * dec_seq
    tile_batch = SAMPLE_TILE // dec_seq
    pool_a, pool_b = _pool_matrices(dec_seq, tile_batch)
    xs = x_sample.reshape(rows, D_MODEL)
    pstate = jnp.pad(state_pool[l], ((0, 0), (1, 0), (0, 0))).reshape(dec_batch * POOL_HIST, D_POOL)
    hist = jnp.pad(state_ffn_conv[l], ((0, 0), (dec_seq - (CONV_K - 1), 0), (0, 0))).reshape(rows, D_FF)
    sguw_t = jnp.tile(sgu_w[l, :, :dec_seq, :dec_seq], (1, tile_batch, tile_batch))
    sgub_s = jnp.tile(jnp.repeat(jnp.transpose(sgu_b[l, :, :dec_seq]), D_HEAD, axis=1), (tile_batch, 1))
    s_inputs = [xs, pstate, hist, shared["g1"], shared["w_in"], shared["pw"], shared["pscale"],
                shared["wpo"], shared["sgug"], sguw_t, sgub_s, shared["wso"], shared["wo"], shared["g2"],
                shared["wup"], shared["wgate"], shared["cw"], shared["cb"], shared["wdown"], shared["gf"],
                pool_a, pool_b]
    s_specs = [pl.BlockSpec((SAMPLE_TILE, D_MODEL), lambda i: (i, 0)),
               pl.BlockSpec((tile_batch * POOL_HIST, D_POOL), lambda i: (i, 0)),
               pl.BlockSpec((SAMPLE_TILE, D_FF), lambda i: (i, 0))]
    s_specs += [_resident(a.shape) for a in s_inputs[3:]]
    tiled = lambda n: pl.BlockSpec((SAMPLE_TILE, n), lambda i: (i, 0))
    y_s, p_s, a_s, v_s = pl.pallas_call(
        functools.partial(_sample_kernel, dec_seq=dec_seq),
        grid=(rows // SAMPLE_TILE,),
        in_specs=s_specs,
        out_specs=[tiled(D_MODEL), tiled(D_POOL), tiled(D_FF), tiled(D_SGU)],
        out_shape=[jax.ShapeDtypeStruct((rows, D_MODEL), F32),
                   jax.ShapeDtypeStruct((rows, D_POOL), F32),
                   jax.ShapeDtypeStruct((rows, D_FF), F32),
                   jax.ShapeDtypeStruct((rows, D_SGU), F32)],
        compiler_params=pltpu.CompilerParams(dimension_semantics=("arbitrary",),
                                             vmem_limit_bytes=VMEM_LIMIT_BYTES),
        name="sample_layer",
    )(*s_inputs)
    y_sample = y_s.reshape(dec_batch, dec_seq, D_MODEL)
    p_new = p_s.reshape(dec_batch, dec_seq, D_POOL)
    new_pool_sample = jnp.concatenate([state_pool[l], p_new], axis=1)[None, :, -POOL_PAD:]
    new_conv_sample = a_s.reshape(dec_batch, dec_seq, D_FF)[None, :, -(CONV_K - 1):]
    new_sgu_v_sample = v_s.reshape(1, dec_batch, dec_seq, D_SGU)
    return (y_prompt, y_sample, new_pool_prompt, new_pool_sample, new_conv_prompt, new_conv_sample,
            new_sgu_v_sample)
```

```python
import functools

import numpy as np
import jax
import jax.numpy as jnp
from jax import lax
from jax.experimental import pallas as pl
from jax.experimental.pallas import tpu as pltpu

D_MODEL = 1024
POOL_WINDOWS = (2, 4, 8, 16)
N_GROUPS = 4
D_GROUP = 128
D_POOL = N_GROUPS * D_GROUP
POOL_PAD = max(POOL_WINDOWS) - 1
POOL_HIST = POOL_PAD + 1
CHUNK = 128
N_HEADS = 4
D_SGU = 512
D_HEAD = D_SGU // N_HEADS
D_IN = D_POOL + 2 * D_SGU + 2 * D_MODEL
D_FF = 2816
CONV_K = 3
EPS = 1e-6

SUBLANES = 8
FF_CHUNK = 256
PROMPT_SUBTILE = 256
PROMPT_TILE = 2 * PROMPT_SUBTILE
SAMPLE_TILE = 256
VMEM_LIMIT_BYTES = 56 * 1024 * 1024

_GELU_C = 0.7978845608028654
_GELU_C3 = _GELU_C * 0.044715

BF16 = jnp.bfloat16
F32 = jnp.float32


def _dot(a, b):
    return jnp.dot(a, b, preferred_element_type=F32)


def _rms(x, g):
    ms = jnp.mean(x * x, axis=-1, keepdims=True)
    return (x * lax.rsqrt(ms + EPS)) * g


def _gelu(x):
    u = x * (_GELU_C + _GELU_C3 * (x * x))
    return x * (0.5 + 0.5 * jnp.tanh(u))


def _sigmoid(x):
    return 0.5 * jnp.tanh(0.5 * x) + 0.5


def _split_bf16(x):
    hi = x.astype(BF16)
    lo = (x - hi.astype(F32)).astype(BF16)
    return hi, lo


def _input_proj(h, w_in_ref, sgug_ref):
    p = _dot(h, w_in_ref[:, 0:D_POOL])
    u = _gelu(_dot(h, w_in_ref[:, D_POOL:D_POOL + D_SGU]))
    v = _rms(_gelu(_dot(h, w_in_ref[:, D_POOL + D_SGU:D_POOL + 2 * D_SGU])), sgug_ref[...])
    g_lo = D_POOL + 2 * D_SGU
    ga = _sigmoid(_dot(h, w_in_ref[:, g_lo:g_lo + D_MODEL]))
    gb = _sigmoid(_dot(h, w_in_ref[:, g_lo + D_MODEL:g_lo + 2 * D_MODEL]))
    return p, u, v, ga, gb


def _pool_groups(d, pw_ref, pscale_ref):
    y01 = _dot(d[:, 0:2 * D_GROUP], pw_ref[0])
    y23 = _dot(d[:, 2 * D_GROUP:], pw_ref[1])
    return (jnp.concatenate([y01, y23], axis=1) * pscale_ref[...]).astype(BF16)


def _mixer_merge(x, ga, gb, y, mix, u, wpo_ref, wso_ref, wo_ref):
    a_out = _dot(y, wpo_ref[...])
    b_out = _dot((u * mix).astype(BF16), wso_ref[...])
    m = (ga * a_out + gb * b_out).astype(BF16)
    return x + _dot(m, wo_ref[...])


def _prompt_kernel(x_ref, g1_ref, w_in_ref, pw_ref, pscale_ref, wpo_ref, sgug_ref, sguw_ref,
                   sgub_ref, wso_ref, wo_ref, g2_ref, wup_ref, wgate_ref, cw_ref, cb_ref,
                   wdown_ref, gf_ref, invw_ref, tbl_ref,
                   y_ref, npool_ref, nconv_ref, carry_p, carry_a):
    st = PROMPT_SUBTILE
    n_sub = PROMPT_TILE // st
    t_idx = pl.program_id(1)

    @pl.when(t_idx == 0)
    def _():
        carry_p[...] = jnp.zeros_like(carry_p)
        carry_a[...] = jnp.zeros_like(carry_a)

    row = lax.broadcasted_iota(jnp.int32, (CHUNK, CHUNK), 0)
    col = lax.broadcasted_iota(jnp.int32, (CHUNK, CHUNK), 1)
    w_s = [jnp.where(row >= col, sguw_ref[hh], 0.0).astype(BF16) for hh in range(N_HEADS)]

    def mixer(x, p_hist, first):
        h = _rms(x, g1_ref[...]).astype(BF16)
        p, u, v, ga, gb = _input_proj(h, w_in_ref, sgug_ref)

        level = jnp.concatenate([p_hist, p], axis=0)
        sums = []
        for g, w in enumerate(POOL_WINDOWS):
            level = level + pltpu.roll(level, w // 2, axis=0)
            sums.append(level[:, 0:D_GROUP])
            if g + 1 < N_GROUPS:
                level = level[:, D_GROUP:]
        win = jnp.concatenate(sums, axis=1)[POOL_HIST:]
        scaled = jnp.concatenate([win[:POOL_HIST] * first, win[POOL_HIST:] * invw_ref[...]], axis=0)
        y = _pool_groups((scaled - p).astype(BF16), pw_ref, pscale_ref)

        vb = v.astype(BF16)
        n_chunks = st // CHUNK
        per_head = []
        for hh in range(N_HEADS):
            rhs = jnp.concatenate(
                [vb[c * CHUNK:(c + 1) * CHUNK, hh * D_HEAD:(hh + 1) * D_HEAD] for c in range(n_chunks)],
                axis=1)
            res = _dot(w_s[hh], rhs)
            bias = sgub_ref[:, hh * D_HEAD:(hh + 1) * D_HEAD]
            per_head.append([res[:, c * D_HEAD:(c + 1) * D_HEAD] + bias for c in range(n_chunks)])
        mix = jnp.concatenate(
            [jnp.concatenate([per_head[hh][c] for hh in range(N_HEADS)], axis=1) for c in range(n_chunks)],
            axis=0)
        return _mixer_merge(x, ga, gb, y, mix, u, wpo_ref, wso_ref, wo_ref), p[st - POOL_HIST:]

    def conv_ffn(x1, a_hist):
        h2 = _rms(x1, g2_ref[...]).astype(BF16)
        f_parts, a_tail = [], []
        for j in range(D_FF // FF_CHUNK):
            lo, hi = j * FF_CHUNK, (j + 1) * FF_CHUNK
            a = _dot(h2, wup_ref[:, lo:hi])
            ext_a = jnp.concatenate([a_hist[j], a], axis=0)
            s1 = pltpu.roll(ext_a, 1, axis=0)[SUBLANES:]
            s2 = pltpu.roll(ext_a, 2, axis=0)[SUBLANES:]
            c = s2 * cw_ref[0:1, lo:hi] + s1 * cw_ref[1:2, lo:hi] + a * cw_ref[2:3, lo:hi]
            a_tail.append(a[st - SUBLANES:])
            f_parts.append((_gelu(c + cb_ref[:, lo:hi]) * _dot(h2, wgate_ref[:, lo:hi])).astype(BF16))
        x2 = x1 + _dot(jnp.concatenate(f_parts, axis=1), wdown_ref[...])
        return _rms(x2, gf_ref[...]), a_tail

    invw_rows = jnp.broadcast_to(invw_ref[...], tbl_ref.shape)
    p_hist = carry_p[...]
    x1s = []
    for s in range(n_sub):
        first = jnp.where(t_idx == 0, tbl_ref[...], invw_rows) if s == 0 else invw_rows
        x1, p_hist = mixer(x_ref[0, s * st:(s + 1) * st], p_hist, first)
        x1s.append(x1)
    carry_p[...] = p_hist
    npool_ref[0] = p_hist

    a_hist = [carry_a[:, j * FF_CHUNK:(j + 1) * FF_CHUNK] for j in range(D_FF // FF_CHUNK)]
    for s in range(n_sub):
        y_out, a_hist = conv_ffn(x1s[s], a_hist)
        y_ref[0, s * st:(s + 1) * st] = y_out
    for j in range(D_FF // FF_CHUNK):
        carry_a[:, j * FF_CHUNK:(j + 1) * FF_CHUNK] = a_hist[j]
        nconv_ref[0, :, j * FF_CHUNK:(j + 1) * FF_CHUNK] = a_hist[j]


def _sample_kernel(x_ref, pstate_ref, hist_ref, g1_ref, w_in_ref, pw_ref, pscale_ref, wpo_ref,
                   sgug_ref, sguwt_ref, sgub_ref, wso_ref, wo_ref, g2_ref, wup_ref, wgate_ref,
                   cw_ref, cb_ref, wdown_ref, gf_ref, pool_a_ref, pool_b_ref,
                   y_ref, p_ref, a_ref, v_ref, *, dec_seq):
    rt = SAMPLE_TILE
    x = x_ref[...]
    h = _rms(x, g1_ref[...]).astype(BF16)

    p, u, v, ga, gb = _input_proj(h, w_in_ref, sgug_ref)
    p_ref[...] = p
    v_ref[...] = v

    p_hi, p_lo = _split_bf16(p)
    s_hi, s_lo = _split_bf16(pstate_ref[...])
    means = []
    for g in range(N_GROUPS):
        sl = slice(g * D_GROUP, (g + 1) * D_GROUP)
        new2 = jnp.concatenate([p_hi[:, sl], p_lo[:, sl]], axis=1)
        old2 = jnp.concatenate([s_hi[:, sl], s_lo[:, sl]], axis=1)
        r = _dot(pool_a_ref[g], new2) + _dot(pool_b_ref[g], old2)
        means.append(r[:, :D_GROUP] + r[:, D_GROUP:])
    d = (jnp.concatenate(means, axis=1) - p).astype(BF16)
    y = _pool_groups(d, pw_ref, pscale_ref)

    vb = v.astype(BF16)
    row = lax.broadcasted_iota(jnp.int32, (rt, rt), 0)
    col = lax.broadcasted_iota(jnp.int32, (rt, rt), 1)
    keep = ((row // dec_seq) == (col // dec_seq)) & (row >= col)
    mixes = []
    for hh in range(N_HEADS):
        w_s = jnp.where(keep, sguwt_ref[hh], 0.0).astype(BF16)
        mixes.append(_dot(w_s, vb[:, hh * D_HEAD:(hh + 1) * D_HEAD]))
    mix = jnp.concatenate(mixes, axis=1) + sgub_ref[...]

    x1 = _mixer_merge(x, ga, gb, y, mix, u, wpo_ref, wso_ref, wo_ref)

    h2 = _rms(x1, g2_ref[...]).astype(BF16)
    tok =lax.broadcasted_iota(jnp.int32, (rt, FF_CHUNK), 0) % dec_seq
    f_parts = []
    for j in range(D_FF // FF_CHUNK):
        lo, hi = j * FF_CHUNK, (j + 1) * FF_CHUNK
        a = _dot(h2, wup_ref[:, lo:hi])
        a_ref[:, lo:hi] = a
        hist = hist_ref[:, lo:hi]
        s1 = jnp.where(tok < 1, pltpu.roll(hist, rt - (dec_seq - 1), axis=0), pltpu.roll(a, 1, axis=0))
        s2 = jnp.where(tok < 2, pltpu.roll(hist, rt - (dec_seq - 2), axis=0), pltpu.roll(a, 2, axis=0))
        c = s2 * cw_ref[0:1, lo:hi] + s1 * cw_ref[1:2, lo:hi] + a * cw_ref[2:3, lo:hi]
        f_parts.append((_gelu(c + cb_ref[:, lo:hi]) * _dot(h2, wgate_ref[:, lo:hi])).astype(BF16))
    x2 = x1 + _dot(jnp.concatenate(f_parts, axis=1), wdown_ref[...])
    y_ref[...] = _rms(x2, gf_ref[...])


def _resident(shape):
    zeros = (0,) * len(shape)
    return pl.BlockSpec(shape, lambda *_: zeros, pipeline_mode=pl.Buffered(1))


def _pool_matrices(dec_seq, n_batch):
    a = np.zeros((N_GROUPS, dec_seq, dec_seq), np.float32)
    b = np.zeros((N_GROUPS, dec_seq, POOL_HIST), np.float32)
    for g, w in enumerate(POOL_WINDOWS):
        for t in range(dec_seq):
            for k in range(w):
                i = POOL_PAD + t - k
                if i >= POOL_PAD:
                    a[g, t, i - POOL_PAD] = 1.0 / w
                else:
                    b[g, t, i + 1] = 1.0 / w
    eye = np.eye(n_batch, dtype=np.float32)
    a_bd = np.stack([np.kron(eye, a[g]) for g in range(N_GROUPS)])
    b_bd = np.stack([np.kron(eye, b[g]) for g in range(N_GROUPS)])
    return jnp.asarray(a_bd, BF16), jnp.asarray(b_bd, BF16)


def _first_rows_table():
    t = np.arange(POOL_HIST, dtype=np.float32)[:, None]
    w = np.repeat(np.asarray(POOL_WINDOWS, np.float32), D_GROUP)[None, :]
    return jnp.asarray(1.0 / np.minimum(w, t + 1.0), F32), jnp.asarray(1.0 / w, F32)


def kernel(x_prompt, x_sample, state_pool, state_ffn_conv, norm1_g, w_in, pool_w, pool_scale,
           w_pool_out, sgu_norm_g, sgu_w, sgu_b, w_sgu_out, w_o, norm2_g, ffn_w_up, ffn_w_gate,
           ffn_conv_w, ffn_conv_b, ffn_w_down, final_norm_g):
    depth = norm1_g.shape[0]
    assert depth == 1
    batch, seq, _ = x_prompt.shape
    dec_batch, dec_seq, _ = x_sample.shape
    assert seq % PROMPT_TILE == 0 and PROMPT_SUBTILE % CHUNK == 0 and PROMPT_SUBTILE >= 2 * POOL_HIST
    assert SAMPLE_TILE % dec_seq == 0 and (dec_batch * dec_seq) % SAMPLE_TILE == 0
    assert CONV_K - 1 <= dec_seq <= CHUNK

    l = 0
    row = lambda v: v.reshape(1, -1)
    zeros_g = jnp.zeros((D_GROUP, D_GROUP), F32)
    pw = jnp.stack([
        jnp.block([[pool_w[l, 0], zeros_g], [zeros_g, pool_w[l, 1]]]),
        jnp.block([[pool_w[l, 2], zeros_g], [zeros_g, pool_w[l, 3]]]),
    ]).astype(BF16)
    shared = dict(
        g1=row(norm1_g[l]), w_in=w_in[l].astype(BF16), pw=pw, pscale=row(pool_scale[l]),
        wpo=w_pool_out[l].astype(BF16), sgug=row(sgu_norm_g[l]), wso=w_sgu_out[l].astype(BF16),
        wo=w_o[l].astype(BF16), g2=row(norm2_g[l]), wup=ffn_w_up[l].astype(BF16),
        wgate=ffn_w_gate[l].astype(BF16), cw=ffn_conv_w[l], cb=row(ffn_conv_b[l]),
        wdown=ffn_w_down[l].astype(BF16), gf=row(final_norm_g))

    tbl, invw = _first_rows_table()
    sgub_p = jnp.repeat(jnp.transpose(sgu_b[l, :, :CHUNK]), D_HEAD, axis=1)
    p_inputs = [x_prompt, shared["g1"], shared["w_in"], shared["pw"], shared["pscale"], shared["wpo"],
                shared["sgug"], sgu_w[l, :, :CHUNK, :CHUNK], sgub_p, shared["wso"], shared["wo"],
                shared["g2"], shared["wup"], shared["wgate"], shared["cw"], shared["cb"],
                shared["wdown"], shared["gf"], invw, tbl]
    p_specs = [pl.BlockSpec((1, PROMPT_TILE, D_MODEL), lambda b, t: (b, t, 0))]
    p_specs += [_resident(a.shape) for a in p_inputs[1:]]
    y_prompt, npool_p, nconv_p = pl.pallas_call(
        _prompt_kernel,
        grid=(batch, seq // PROMPT_TILE),
        in_specs=p_specs,
        out_specs=[pl.BlockSpec((1, PROMPT_TILE, D_MODEL), lambda b, t: (b, t, 0)),
                   pl.BlockSpec((1, POOL_HIST, D_POOL), lambda b, t: (b, 0, 0)),
                   pl.BlockSpec((1, SUBLANES, D_FF), lambda b, t: (b, 0, 0))],
        out_shape=[jax.ShapeDtypeStruct((batch, seq, D_MODEL), F32),
                   jax.ShapeDtypeStruct((batch, POOL_HIST, D_POOL), F32),
                   jax.ShapeDtypeStruct((batch, SUBLANES, D_FF), F32)],
        scratch_shapes=[pltpu.VMEM((POOL_HIST, D_POOL), F32), pltpu.VMEM((SUBLANES, D_FF), F32)],
        compiler_params=pltpu.CompilerParams(dimension_semantics=("arbitrary", "arbitrary"),
                                             vmem_limit_bytes=VMEM_LIMIT_BYTES),
        name="prompt_layer",
    )(*p_inputs)
    new_pool_prompt = npool_p[None, :, POOL_HIST - POOL_PAD:]
    new_conv_prompt = nconv_p[None, :, SUBLANES - (CONV_K - 1):]

    rows = dec_batch * dec_seq
    tile_batch = SAMPLE_TILE // dec_seq
    pool_a, pool_b = _pool_matrices(dec_seq, tile_batch)
    xs = x_sample.reshape(rows, D_MODEL)
    pstate = jnp.pad(state_pool[l], ((0, 0), (1, 0), (0, 0))).reshape(dec_batch * POOL_HIST, D_POOL)
    hist = jnp.pad(state_ffn_conv[l], ((0, 0), (dec_seq - (CONV_K - 1), 0), (0, 0))).reshape(rows, D_FF)
    sguw_t = jnp.tile(sgu_w[l, :, :dec_seq, :dec_seq], (1, tile_batch, tile_batch))
    sgub_s = jnp.tile(jnp.repeat(jnp.transpose(sgu_b[l, :, :dec_seq]), D_HEAD, axis=1), (tile_batch, 1))
    s_inputs = [xs, pstate, hist, shared["g1"], shared["w_in"], shared["pw"], shared["pscale"],
                shared["wpo"], shared["sgug"], sguw_t, sgub_s, shared["wso"], shared["wo"], shared["g2"],
                shared["wup"], shared["wgate"], shared["cw"], shared["cb"], shared["wdown"], shared["gf"],
                pool_a, pool_b]
    s_specs = [pl.BlockSpec((SAMPLE_TILE, D_MODEL), lambda i: (i, 0)),
               pl.BlockSpec((tile_batch * POOL_HIST, D_POOL), lambda i: (i, 0)),
               pl.BlockSpec((SAMPLE_TILE, D_FF), lambda i: (i, 0))]
    s_specs += [_resident(a.shape) for a in s_inputs[3:]]
    tiled = lambda n: pl.BlockSpec((SAMPLE_TILE, n), lambda i: (i, 0))
    y_s, p_s, a_s, v_s = pl.pallas_call(
        functools.partial(_sample_kernel, dec_seq=dec_seq),
        grid=(rows // SAMPLE_TILE,),
        in_specs=s_specs,
        out_specs=[tiled(D_MODEL), tiled(D_POOL), tiled(D_FF), tiled(D_SGU)],
        out_shape=[jax.ShapeDtypeStruct((rows, D_MODEL), F32),
                   jax.ShapeDtypeStruct((rows, D_POOL), F32),
                   jax.ShapeDtypeStruct((rows, D_FF), F32),
                   jax.ShapeDtypeStruct((rows, D_SGU), F32)],
        compiler_params=pltpu.CompilerParams(dimension_semantics=("arbitrary",),
                                             vmem_limit_bytes=VMEM_LIMIT_BYTES),
        name="sample_layer",
    )(*s_inputs)
    y_sample = y_s.reshape(dec_batch, dec_seq, D_MODEL)
    p_new = p_s.reshape(dec_batch, dec_seq, D_POOL)
    new_pool_sample = jnp.concatenate([state_pool[l], p_new], axis=1)[None, :, -POOL_PAD:]
    new_conv_sample = a_s.reshape(dec_batch, dec_seq, D_FF)[None, :, -(CONV_K - 1):]
    new_sgu_v_sample = v_s.reshape(1, dec_batch, dec_seq, D_SGU)
    return (y_prompt, y_sample, new_pool_prompt, new_pool_sample, new_conv_prompt, new_conv_sample,
            new_sgu_v_sample)
```

```python
import functools

import numpy as np
import jax
import jax.numpy as jnp
from jax import lax
from jax.experimental import pallas as pl
from jax.experimental.pallas import tpu as pltpu

D_MODEL = 1024
POOL_WINDOWS = (2, 4, 8, 16)
N_GROUPS = 4
D_GROUP = 128
D_POOL = N_GROUPS * D_GROUP
POOL_PAD = max(POOL_WINDOWS) - 1
POOL_HIST = POOL_PAD + 1
CHUNK = 128
N_HEADS = 4
D_SGU = 512
D_HEAD = D_SGU // N_HEADS
D_IN = D_POOL + 2 * D_SGU + 2 * D_MODEL
D_FF = 2816
CONV_K = 3
EPS = 1e-6

SUBLANES = 8
LANES = 128
FF_CHUNK = 256
N_FF_CHUNKS = D_FF // FF_CHUNK
PROMPT_SUBTILE = 256
PROMPT_TILE = 2 * PROMPT_SUBTILE
SAMPLE_TILE = 256
WIDE_ROWS = 64
NARROW_ROWS = 256
VMEM_LIMIT_BYTES = 60 * 1024 * 1024

_GELU_C = 0.7978845608028654
_GELU_C3 = _GELU_C * 0.044715

BF16 = jnp.bfloat16
F32 = jnp.float32


def _dot(a, b):
    return jnp.dot(a, b, preferred_element_type=F32)


def _rms(x, g):
    ms = jnp.mean(x * x, axis=-1, keepdims=True)
    return (x * lax.rsqrt(ms + EPS)) * g


def _gelu(x):
    u = x * (_GELU_C + _GELU_C3 * (x * x))
    return x * (0.5 + 0.5 * jnp.tanh(u))


def _sigmoid(x):
    return 0.5 * jnp.tanh(0.5 * x) + 0.5


def _split_bf16(x):
    hi = x.astype(BF16)
    lo = (x - hi.astype(F32)).astype(BF16)
    return hi, lo


def _input_proj(h, w_in_ref, sgug_ref):
    p = _dot(h, w_in_ref[:, 0:D_POOL])
    u = _gelu(_dot(h, w_in_ref[:, D_POOL:D_POOL + D_SGU]))
    v = _rms(_gelu(_dot(h, w_in_ref[:, D_POOL + D_SGU:D_POOL + 2 * D_SGU])), sgug_ref[...])
    g_lo = D_POOL + 2 * D_SGU
    ga = _sigmoid(_dot(h, w_in_ref[:, g_lo:g_lo + D_MODEL]))
    gb = _sigmoid(_dot(h, w_in_ref[:, g_lo + D_MODEL:g_lo + 2 * D_MODEL]))
    return p, u, v, ga, gb


def _pool_groups(d, pw_ref, pscale_ref):
    y01 = _dot(d[:, 0:2 * D_GROUP], pw_ref[0])
    y23 = _dot(d[:, 2 * D_GROUP:], pw_ref[1])
    return (jnp.concatenate([y01, y23], axis=1) * pscale_ref[...]).astype(BF16)


def _mixer_merge(x, ga, gb, y, mix, u, wpo_ref, wso_ref, wo_ref):
    a_out = _dot(y, wpo_ref[...])
    b_out = _dot((u * mix).astype(BF16), wso_ref[...])
    m = (ga * a_out + gb * b_out).astype(BF16)
    return x + _dot(m, wo_ref[...])


def _load_weight_group(weights, stage, sem):
    def copy(k, c, slot):
        hbm, _, _, cols, ch = weights[k]
        return pltpu.make_async_copy(hbm.at[0, pl.ds(c * ch, ch), :],
                                     stage.at[slot, pl.ds(0, ch), pl.ds(0, cols)], sem.at[slot])

    copy(0, 0, 0).start()
    base = 0
    for k, (_, vmem, rows, cols, ch) in enumerate(weights):
        n = rows // ch

        def body(c, carry, k=k, vmem=vmem, cols=cols, ch=ch, n=n, base=base):
            slot = (base + c) % 2

            @pl.when(c + 1 < n)
            def _():
                copy(k, c + 1, 1 - slot).start()

            if k + 1 < len(weights):
                @pl.when(c + 1 == n)
                def _():
                    copy(k + 1, 0, 1 - slot).start()

            copy(k, c, slot).wait()
            vmem[pl.ds(pl.multiple_of(c * ch, ch), ch), :] = stage[slot, 0:ch, 0:cols].astype(BF16)
            return carry

        lax.fori_loop(0, n, body, 0)
        base += n


def _layer_kernel(
        xp_ref, xs_ref, stp_ref, stc_ref,
        g1_ref, pscale_ref, sgug_ref, sgub_ref, g2_ref, cw_ref, cb_ref, gf_ref, sguw_ref, poolw_ref,
        invw_ref, tbl_ref, pool_a_ref, pool_b_ref, onehot_ref,
        w_in_hbm, wpo_hbm, wso_hbm, wo_hbm, wup_hbm, wgate_hbm, wdown_hbm,
        yp_ref, ys_ref, npoolp_ref, nconvp_ref, npools_ref, nconvs_ref, vs_ref,
        w_in_ref, wpo_ref, wso_ref, wo_ref, wup_ref, wgate_ref, wdown_ref, pw_ref, wsm_ref, bias_ref,
        carry_p, carry_a, stage_wide, stage_narrow, sem_wide, sem_narrow,
        *, n_prompt_steps, tiles_per_seq, dec_seq):
    step = pl.program_id(0)

    @pl.when(step == 0)
    def _prepare():
        _load_weight_group(
            [(w_in_hbm, w_in_ref, D_MODEL, D_IN, WIDE_ROWS),
             (wup_hbm, wup_ref, D_MODEL, D_FF, WIDE_ROWS),
             (wgate_hbm, wgate_ref, D_MODEL, D_FF, WIDE_ROWS)], stage_wide, sem_wide)
        _load_weight_group(
            [(wpo_hbm, wpo_ref, D_POOL, D_MODEL, NARROW_ROWS),
             (wso_hbm, wso_ref, D_SGU, D_MODEL, NARROW_ROWS),
             (wo_hbm, wo_ref, D_MODEL, D_MODEL, NARROW_ROWS),
             (wdown_hbm, wdown_ref, D_FF, D_MODEL, NARROW_ROWS)], stage_narrow, sem_narrow)
        pw_ref[...] = jnp.zeros_like(pw_ref)
        for g in range(N_GROUPS):
            lo = (g % 2) * D_GROUP
            pw_ref[g // 2, lo:lo + D_GROUP, lo:lo + D_GROUP] = poolw_ref[0, g].astype(BF16)
        row = lax.broadcasted_iota(jnp.int32, (CHUNK, CHUNK), 0)
        col = lax.broadcasted_iota(jnp.int32, (CHUNK, CHUNK), 1)
        for hh in range(N_HEADS):
            wsm_ref[hh] = jnp.where(row >= col, sguw_ref[0, hh], 0.0).astype(BF16)
            b_row = sgub_ref[0, hh:hh + 1, :]
            bias_ref[:, hh * D_HEAD:(hh + 1) * D_HEAD] = jnp.broadcast_to(b_row, (D_HEAD, CHUNK)).T

    def conv_ffn(x1, shifted):
        h2 = _rms(x1, g2_ref[...]).astype(BF16)
        f_parts, a_parts = [], []
        for j in range(N_FF_CHUNKS):
            lo, hi = j * FF_CHUNK, (j + 1) * FF_CHUNK
            a = _dot(h2, wup_ref[:, lo:hi])
            s1, s2 = shifted(j, a)
            c = s2 * cw_ref[0, 0:1, lo:hi] + s1 * cw_ref[0, 1:2, lo:hi] + a * cw_ref[0, 2:3, lo:hi]
            a_parts.append(a)
            f_parts.append((_gelu(c + cb_ref[:, lo:hi]) * _dot(h2, wgate_ref[:, lo:hi])).astype(BF16))
        x2 = x1 + _dot(jnp.concatenate(f_parts, axis=1), wdown_ref[...])
        return _rms(x2, gf_ref[...]), a_parts

    @pl.when(step < n_prompt_steps)
    def _prompt():
        st = PROMPT_SUBTILE
        n_sub = PROMPT_TILE // st
        first_tile = (step % tiles_per_seq) == 0

        @pl.when(first_tile)
        def _():
            carry_p[...] = jnp.zeros_like(carry_p)
            carry_a[...] = jnp.zeros_like(carry_a)

        def mixer(x, p_hist, first):
            h = _rms(x, g1_ref[...]).astype(BF16)
            p, u, v, ga, gb = _input_proj(h, w_in_ref, sgug_ref)

            level = jnp.concatenate([p_hist, p], axis=0)
            sums = []
            for g, w in enumerate(POOL_WINDOWS):
                level = level + pltpu.roll(level, w // 2, axis=0)
                sums.append(level[:, 0:D_GROUP])
                if g + 1 < N_GROUPS:
                    level = level[:, D_GROUP:]
            win = jnp.concatenate(sums, axis=1)[POOL_HIST:]
            scaled = jnp.concatenate([win[:POOL_HIST] * first, win[POOL_HIST:] * invw_ref[...]], axis=0)
            y = _pool_groups((scaled - p).astype(BF16), pw_ref, pscale_ref)

            vb = v.astype(BF16)
            n_chunks = st // CHUNK
            per_head = []
            for hh in range(N_HEADS):
                rhs = jnp.concatenate(
                    [vb[c * CHUNK:(c + 1) * CHUNK, hh * D_HEAD:(hh + 1) * D_HEAD] for c in range(n_chunks)],
                    axis=1)
                res = _dot(wsm_ref[hh], rhs)
                bias = bias_ref[:, hh * D_HEAD:(hh + 1) * D_HEAD]
                per_head.append([res[:, c * D_HEAD:(c + 1) * D_HEAD] + bias for c in range(n_chunks)])
            mix = jnp.concatenate(
                [jnp.concatenate([per_head[hh][c] for hh in range(N_HEADS)], axis=1) for c in range(n_chunks)],
                axis=0)
            return _mixer_merge(x, ga, gb, y, mix, u, wpo_ref, wso_ref, wo_ref), p[st - POOL_HIST:]

        invw_rows = jnp.broadcast_to(invw_ref[...], tbl_ref.shape)
        p_hist = carry_p[...]
        x1s = []
        for s in range(n_sub):
            first = jnp.where(first_tile, tbl_ref[...], invw_rows) if s == 0 else invw_rows
            x1, p_hist = mixer(xp_ref[0, s * st:(s + 1) * st], p_hist, first)
            x1s.append(x1)
        carry_p[...] = p_hist
        npoolp_ref[0, 0] = p_hist[POOL_HIST - POOL_PAD:]

        a_hist = [carry_a[:, j * FF_CHUNK:(j + 1) * FF_CHUNK] for j in range(N_FF_CHUNKS)]
        for s in range(n_sub):
            def shifted(j, a, a_hist=a_hist):
                ext = jnp.concatenate([a_hist[j], a], axis=0)
                return pltpu.roll(ext, 1, axis=0)[SUBLANES:], pltpu.roll(ext, 2, axis=0)[SUBLANES:]

            y_out, a_parts = conv_ffn(x1s[s], shifted)
            a_hist = [a[st - SUBLANES:] for a in a_parts]
            yp_ref[0, s * st:(s + 1) * st] = y_out
        for j in range(N_FF_CHUNKS):
            carry_a[:, j * FF_CHUNK:(j + 1) * FF_CHUNK] = a_hist[j]
            nconvp_ref[0, 0, :, j * FF_CHUNK:(j + 1) * FF_CHUNK] = a_hist[j][SUBLANES - (CONV_K - 1):]

    @pl.when(step >= n_prompt_steps)
    def _sample():
        rt = SAMPLE_TILE
        nb = rt // dec_seq
        x = xs_ref[...]
        h = _rms(x, g1_ref[...]).astype(BF16)
        p, u, v, ga, gb = _input_proj(h, w_in_ref, sgug_ref)
        vs_ref[...] = v

        stp = stp_ref[...]
        old = jnp.concatenate([stp[:, i * D_POOL:(i + 1) * D_POOL] for i in range(POOL_PAD)]
                              + [jnp.zeros((nb, D_POOL), F32)], axis=0)
        p_hi, p_lo = _split_bf16(p)
        s_hi, s_lo = _split_bf16(old)
        means = []
        for g in range(N_GROUPS):
            sl = slice(g * D_GROUP, (g + 1) * D_GROUP)
            new2 = jnp.concatenate([p_hi[:, sl], p_lo[:, sl]], axis=1)
            old2 = jnp.concatenate([s_hi[:, sl], s_lo[:, sl]], axis=1)
            r = _dot(pool_a_ref[g], new2) + _dot(pool_b_ref[g], old2)
            means.append(r[:, :D_GROUP] + r[:, D_GROUP:])
        y = _pool_groups((jnp.concatenate(means, axis=1) - p).astype(BF16), pw_ref, pscale_ref)
        p3 = p.reshape(nb, dec_seq, D_POOL)
        for i in range(POOL_PAD):
            k = i + dec_seq
            src = stp[:, k * D_POOL:(k + 1) * D_POOL] if k < POOL_PAD else p3[:, k - POOL_PAD, :]
            npools_ref[:, i * D_POOL:(i + 1) * D_POOL] = src

        vb = v.astype(BF16)
        row = lax.broadcasted_iota(jnp.int32, (rt, rt), 0)
        col = lax.broadcasted_iota(jnp.int32, (rt, rt), 1)
        keep = ((row // dec_seq) == (col // dec_seq)) & (row >= col)
        mixes = []
        for hh in range(N_HEADS):
            w_rows = jnp.tile(sguw_ref[0, hh, 0:dec_seq, :], (nb, 1)).astype(BF16)
            w_s = jnp.where(keep, _dot(w_rows, onehot_ref[...]), 0.0).astype(BF16)
            bias = jnp.tile(bias_ref[0:dec_seq, hh * D_HEAD:(hh + 1) * D_HEAD], (nb, 1))
            mixes.append(_dot(w_s, vb[:, hh * D_HEAD:(hh + 1) * D_HEAD]) + bias)
        mix = jnp.concatenate(mixes, axis=1)

        x1 = _mixer_merge(x, ga, gb, y, mix, u, wpo_ref, wso_ref, wo_ref)

        tok = lax.broadcasted_iota(jnp.int32, (rt, FF_CHUNK), 0) % dec_seq

        def shifted(j, a):
            lo, hi = j * FF_CHUNK, (j + 1) * FF_CHUNK
            per_row = lambda r: jnp.broadcast_to(r[:, None, :], (nb, dec_seq, FF_CHUNK)).reshape(rt, FF_CHUNK)
            back2 = per_row(stc_ref[:, lo:hi])
            back1 = per_row(stc_ref[:, D_FF + lo:D_FF + hi])
            s1 = jnp.where(tok == 0, back1, pltpu.roll(a, 1, axis=0))
            s2 = jnp.where(tok == 0, back2, jnp.where(tok == 1, back1, pltpu.roll(a, 2, axis=0)))
            return s1, s2

        y_out, a_parts = conv_ffn(x1, shifted)
        ys_ref[...] = y_out
        for j, a in enumerate(a_parts):
            a3 = a.reshape(nb, dec_seq, FF_CHUNK)
            for r in range(CONV_K - 1):
                lo = r * D_FF + j * FF_CHUNK
                nconvs_ref[:, lo:lo + FF_CHUNK] = a3[:, dec_seq - (CONV_K - 1) + r, :]


def _pool_matrices(dec_seq, n_batch):
    a = np.zeros((N_GROUPS, n_batch * dec_seq, n_batch * dec_seq), np.float32)
    b = np.zeros((N_GROUPS, n_batch * dec_seq, POOL_HIST * n_batch), np.float32)
    for g, w in enumerate(POOL_WINDOWS):
        for bb in range(n_batch):
            for t in range(dec_seq):
                for k in range(w):
                    i = POOL_PAD + t - k
                    if i >= POOL_PAD:
                        a[g, bb * dec_seq + t, bb * dec_seq + i - POOL_PAD] = 1.0 / w
                    else:
                        b[g, bb * dec_seq + t, i * n_batch + bb] = 1.0 / w
    return jnp.asarray(a, BF16), jnp.asarray(b, BF16)


def _first_rows_table():
    t = np.arange(POOL_HIST, dtype=np.float32)[:, None]
    w = np.repeat(np.asarray(POOL_WINDOWS, np.float32), D_GROUP)[None, :]
    return jnp.asarray(1.0 / np.minimum(w, t + 1.0), F32), jnp.asarray(1.0 / w, F32)


def _whole(shape):
    zeros = (0,) * len(shape)
    return pl.BlockSpec(shape, lambda i: zeros, pipeline_mode=pl.Buffered(1))


def kernel(x_prompt, x_sample, state_pool, state_ffn_conv, norm1_g, w_in, pool_w, pool_scale,
           w_pool_out, sgu_norm_g, sgu_w, sgu_b, w_sgu_out, w_o, norm2_g, ffn_w_up, ffn_w_gate,
           ffn_conv_w, ffn_conv_b, ffn_w_down, final_norm_g):
    depth = norm1_g.shape[0]
    assert depth == 1
    batch, seq, _ = x_prompt.shape
    dec_batch, dec_seq, _ = x_sample.shape
    assert seq % PROMPT_TILE == 0 and PROMPT_SUBTILE % CHUNK == 0 and PROMPT_SUBTILE >= 2 * POOL_HIST
    assert SAMPLE_TILE % dec_seq == 0 and (dec_batch * dec_seq) % SAMPLE_TILE == 0
    assert CONV_K - 1 <= dec_seq <= CHUNK and dec_seq % SUBLANES == 0
    assert sgu_w.shape[-1] == CHUNK and state_pool.shape[2] == POOL_PAD

    tiles_per_seq = seq // PROMPT_TILE
    n_p = batch * tiles_per_seq
    rows = dec_batch * dec_seq
    n_s = rows // SAMPLE_TILE
    tile_batch = SAMPLE_TILE // dec_seq

    tbl, invw = _first_rows_table()
    pool_a, pool_b = _pool_matrices(dec_seq, tile_batch)
    onehot = np.zeros((CHUNK, SAMPLE_TILE), np.float32)
    onehot[np.arange(SAMPLE_TILE) % dec_seq, np.arange(SAMPLE_TILE)] = 1.0
    onehot = jnp.asarray(onehot, BF16)

    p_tile = lambda i: jnp.minimum(i, n_p - 1)
    s_tile = lambda i: jnp.maximum(i - n_p, 0)
    s_rows = lambda n: pl.BlockSpec((SAMPLE_TILE, n), lambda i: (s_tile(i), 0), pipeline_mode=pl.Buffered(1))
    s_batches = lambda n: pl.BlockSpec((tile_batch, n), lambda i: (s_tile(i), 0), pipeline_mode=pl.Buffered(1))
    hbm = pl.BlockSpec(memory_space=pl.ANY)

    tiled = [x_prompt, x_sample.reshape(rows, D_MODEL), state_pool.reshape(dec_batch, POOL_PAD * D_POOL),
             state_ffn_conv.reshape(dec_batch, (CONV_K - 1) * D_FF)]
    tiled_specs = [
        pl.BlockSpec((1, PROMPT_TILE, D_MODEL), lambda i: (p_tile(i) // tiles_per_seq, p_tile(i) % tiles_per_seq, 0)),
        s_rows(D_MODEL), s_batches(POOL_PAD * D_POOL), s_batches((CONV_K - 1) * D_FF)]
    small = [norm1_g, pool_scale, sgu_norm_g, sgu_b, norm2_g, ffn_conv_w, ffn_conv_b,
             final_norm_g.reshape(1, D_MODEL), sgu_w, pool_w, invw, tbl, pool_a, pool_b, onehot]
    big = [w_in, w_pool_out, w_sgu_out, w_o, ffn_w_up, ffn_w_gate, ffn_w_down]

    out_shape = [
        jax.ShapeDtypeStruct((batch, seq, D_MODEL), F32),
        jax.ShapeDtypeStruct((rows, D_MODEL), F32),
        jax.ShapeDtypeStruct((1, batch, POOL_PAD, D_POOL), F32),
        jax.ShapeDtypeStruct((1, batch, CONV_K - 1, D_FF), F32),
        jax.ShapeDtypeStruct((dec_batch, POOL_PAD * D_POOL), F32),
        jax.ShapeDtypeStruct((dec_batch, (CONV_K - 1) * D_FF), F32),
        jax.ShapeDtypeStruct((rows, D_SGU), F32)]
    out_specs = [
        pl.BlockSpec((1, PROMPT_TILE, D_MODEL), lambda i: (p_tile(i) // tiles_per_seq, p_tile(i) % tiles_per_seq, 0)),
        s_rows(D_MODEL),
        pl.BlockSpec((1, 1, POOL_PAD, D_POOL), lambda i: (0, p_tile(i) // tiles_per_seq, 0, 0)),
        pl.BlockSpec((1, 1, CONV_K - 1, D_FF), lambda i: (0, p_tile(i) // tiles_per_seq, 0, 0)),
        s_batches(POOL_PAD * D_POOL), s_batches((CONV_K - 1) * D_FF), s_rows(D_SGU)]
    scratch = [
        pltpu.VMEM((D_MODEL, D_IN), BF16), pltpu.VMEM((D_POOL, D_MODEL), BF16), pltpu.VMEM((D_SGU, D_MODEL), BF16),
        pltpu.VMEM((D_MODEL, D_MODEL), BF16), pltpu.VMEM((D_MODEL, D_FF), BF16), pltpu.VMEM((D_MODEL, D_FF), BF16),
        pltpu.VMEM((D_FF, D_MODEL), BF16),
        pltpu.VMEM((N_GROUPS // 2, 2 * D_GROUP, 2 * D_GROUP), BF16),
        pltpu.VMEM((N_HEADS, CHUNK, CHUNK), BF16),
        pltpu.VMEM((CHUNK, D_SGU), F32),
        pltpu.VMEM((POOL_HIST, D_POOL), F32), pltpu.VMEM((SUBLANES, D_FF), F32),
        pltpu.VMEM((2, WIDE_ROWS, D_IN), F32), pltpu.VMEM((2, NARROW_ROWS, D_MODEL), F32),
        pltpu.SemaphoreType.DMA((2,)), pltpu.SemaphoreType.DMA((2,))]

    outs = pl.pallas_call(
        functools.partial(_layer_kernel, n_prompt_steps=n_p, tiles_per_seq=tiles_per_seq, dec_seq=dec_seq),
        grid=(n_p + n_s,),
        in_specs=tiled_specs + [_whole(a.shape) for a in small] + [hbm] * len(big),
        out_specs=out_specs,
        out_shape=out_shape,
        scratch_shapes=scratch,
        compiler_params=pltpu.CompilerParams(dimension_semantics=("arbitrary",),
                                             vmem_limit_bytes=VMEM_LIMIT_BYTES),
        name="layer_step",
    )(*tiled, *small, *big)
    y_prompt, y_s, npool_p, nconv_p, npool_s, nconv_s, v_s = outs
    return (y_prompt, y_s.reshape(dec_batch, dec_seq, D_MODEL),
            npool_p, npool_s.reshape(1, dec_batch, POOL_PAD, D_POOL),
            nconv_p, nconv_s.reshape(1, dec_batch, CONV_K - 1, D_FF),
            v_s.reshape(1, dec_batch, dec_seq, D_SGU))
```

```python
import functools

import numpy as np
import jax
import jax.numpy as jnp
from jax import lax
from jax.experimental import pallas as pl
from jax.experimental.pallas import tpu as pltpu

D_MODEL = 1024
POOL_WINDOWS = (2, 4, 8, 16)
N_GROUPS = 4
D_GROUP = 128
D_POOL = N_GROUPS * D_GROUP
POOL_PAD = max(POOL_WINDOWS) - 1
POOL_HIST = POOL_PAD + 1
CHUNK = 128
N_HEADS = 4
D_SGU = 512
D_HEAD = D_SGU // N_HEADS
D_IN = D_POOL + 2 * D_SGU + 2 * D_MODEL
D_FF = 2816
CONV_K = 3
EPS = 1e-6

SUBLANES = 8
LANES = 128
FF_CHUNK = 256
N_FF_CHUNKS = D_FF // FF_CHUNK
PROMPT_SUBTILE = 256
PROMPT_TILE = 2 * PROMPT_SUBTILE
SAMPLE_TILE = 256
WIDE_ROWS = 128
NARROW_ROWS = 256
LOAD_SLOTS = 4
CAST_ROWS = 16
VMEM_LIMIT_BYTES = 60 * 1024 * 1024

_GELU_C = 0.7978845608028654
_GELU_C3 = _GELU_C * 0.044715

BF16 = jnp.bfloat16
F32 = jnp.float32


def _dot(a, b):
    return jnp.dot(a, b, preferred_element_type=F32)


def _rms(x, g):
    ms = jnp.mean(x * x, axis=-1, keepdims=True)
    return (x * lax.rsqrt(ms + EPS)) * g


def _gelu(x):
    u = x * (_GELU_C + _GELU_C3 * (x * x))
    return x * (0.5 + 0.5 * jnp.tanh(u))


def _sigmoid(x):
    return 0.5 * jnp.tanh(0.5 * x) + 0.5


def _split_bf16(x):
    hi = x.astype(BF16)
    lo = (x - hi.astype(F32)).astype(BF16)
    return hi, lo


def _input_proj(h, w_in_ref, sgug_ref):
    p = _dot(h, w_in_ref[:, 0:D_POOL])
    u = _gelu(_dot(h, w_in_ref[:, D_POOL:D_POOL + D_SGU]))
    v = _rms(_gelu(_dot(h, w_in_ref[:, D_POOL + D_SGU:D_POOL + 2 * D_SGU])), sgug_ref[...])
    g_lo = D_POOL + 2 * D_SGU
    ga = _sigmoid(_dot(h, w_in_ref[:, g_lo:g_lo + D_MODEL]))
    gb = _sigmoid(_dot(h, w_in_ref[:, g_lo + D_MODEL:g_lo + 2 * D_MODEL]))
    return p, u, v, ga, gb


def _pool_groups(d, pw_ref, pscale_ref):
    y01 = _dot(d[:, 0:2 * D_GROUP], pw_ref[0])
    y23 = _dot(d[:, 2 * D_GROUP:], pw_ref[1])
    return (jnp.concatenate([y01, y23], axis=1) * pscale_ref[...]).astype(BF16)


def _mixer_merge(x, ga, gb, y, mix, u, wpo_ref, wso_ref, wo_ref):
    a_out = _dot(y, wpo_ref[...])
    b_out = _dot((u * mix).astype(BF16), wso_ref[...])
    m = (ga * a_out + gb * b_out).astype(BF16)
    return x + _dot(m, wo_ref[...])


def _load_weight_group(weights, chunk_rows, max_cols):
    chunks = [(hbm, vmem, r0, cols)
              for hbm, vmem, rows, cols in weights for r0 in range(0, rows, chunk_rows)]
    lookahead = LOAD_SLOTS - 1

    def run(stage, sem):
        def copy(g):
            hbm, _, r0, cols = chunks[g]
            slot = g % LOAD_SLOTS
            return pltpu.make_async_copy(hbm.at[0, r0:r0 + chunk_rows, :],
                                         stage.at[slot, :, 0:cols], sem.at[slot])

        for g in range(min(lookahead, len(chunks))):
            copy(g).start()
        for g, (_, vmem, r0, cols) in enumerate(chunks):
            if g + lookahead < len(chunks):
                copy(g + lookahead).start()
            copy(g).wait()
            slot = g % LOAD_SLOTS

            def cast(r, carry, vmem=vmem, r0=r0, cols=cols, slot=slot):
                rr = pl.multiple_of(r * CAST_ROWS, CAST_ROWS)
                vmem[pl.ds(r0 + rr, CAST_ROWS), :] = stage[slot, pl.ds(rr, CAST_ROWS), 0:cols].astype(BF16)
                return carry

            lax.fori_loop(0, chunk_rows // CAST_ROWS, cast, 0)

    pl.run_scoped(run, pltpu.VMEM((LOAD_SLOTS, chunk_rows, max_cols), F32),
                  pltpu.SemaphoreType.DMA((LOAD_SLOTS,)))


def _layer_kernel(
        xp_ref, xs_ref, stp_ref, stc_ref,
        g1_ref, pscale_ref, sgug_ref, sgub_ref, g2_ref, cw_ref, cb_ref, gf_ref, sguw_ref, poolw_ref,
        invw_ref, tbl_ref, pool_a_ref, pool_b_ref, onehot_ref,
        w_in_hbm, wpo_hbm, wso_hbm, wo_hbm, wup_hbm, wgate_hbm, wdown_hbm,
        yp_ref, ys_ref, npoolp_ref, nconvp_ref, npools_ref, nconvs_ref, vs_ref,
        w_in_ref, wpo_ref, wso_ref, wo_ref, wup_ref, wgate_ref, wdown_ref, pw_ref, wsm_ref, bias_ref,
        carry_p, carry_a,
        *, n_prompt_steps, tiles_per_seq, dec_seq):
    step = pl.program_id(0)

    @pl.when(step == 0)
    def _prepare():
        _load_weight_group(
            [(w_in_hbm, w_in_ref, D_MODEL, D_IN), (wup_hbm, wup_ref, D_MODEL, D_FF),
             (wgate_hbm, wgate_ref, D_MODEL, D_FF)], WIDE_ROWS, D_IN)
        _load_weight_group(
            [(wdown_hbm, wdown_ref, D_FF, D_MODEL), (wo_hbm, wo_ref, D_MODEL, D_MODEL),
             (wpo_hbm, wpo_ref, D_POOL, D_MODEL), (wso_hbm, wso_ref, D_SGU, D_MODEL)], NARROW_ROWS, D_MODEL)
        pw_ref[...] = jnp.zeros_like(pw_ref)
        for g in range(N_GROUPS):
            lo = (g % 2) * D_GROUP
            pw_ref[g // 2, lo:lo + D_GROUP, lo:lo + D_GROUP] = poolw_ref[0, g].astype(BF16)
        row = lax.broadcasted_iota(jnp.int32, (CHUNK, CHUNK), 0)
        col = lax.broadcasted_iota(jnp.int32, (CHUNK, CHUNK), 1)
        for hh in range(N_HEADS):
            wsm_ref[hh] = jnp.where(row >= col, sguw_ref[0, hh], 0.0).astype(BF16)
            b_row = sgub_ref[0, hh:hh + 1, :]
            bias_ref[:, hh * D_HEAD:(hh + 1) * D_HEAD] = jnp.broadcast_to(b_row, (D_HEAD, CHUNK)).T

    def conv_ffn(x1, shifted):
        h2 = _rms(x1, g2_ref[...]).astype(BF16)
        f_parts, a_parts = [], []
        for j in range(N_FF_CHUNKS):
            lo, hi = j * FF_CHUNK, (j + 1) * FF_CHUNK
            a = _dot(h2, wup_ref[:, lo:hi])
            s1, s2 = shifted(j, a)
            c = s2 * cw_ref[0, 0:1, lo:hi] + s1 * cw_ref[0, 1:2, lo:hi] + a * cw_ref[0, 2:3, lo:hi]
            a_parts.append(a)
            f_parts.append((_gelu(c + cb_ref[:, lo:hi]) * _dot(h2, wgate_ref[:, lo:hi])).astype(BF16))
        x2 = x1 + _dot(jnp.concatenate(f_parts, axis=1), wdown_ref[...])
        return _rms(x2, gf_ref[...]), a_parts

    @pl.when(step < n_prompt_steps)
    def _prompt():
        st = PROMPT_SUBTILE
        n_sub = PROMPT_TILE // st
        first_tile = (step % tiles_per_seq) == 0

        @pl.when(first_tile)
        def _():
            carry_p[...] = jnp.zeros_like(carry_p)
            carry_a[...] = jnp.zeros_like(carry_a)

        def mixer(x, p_hist, first):
            h = _rms(x, g1_ref[...]).astype(BF16)
            p, u, v, ga, gb = _input_proj(h, w_in_ref, sgug_ref)

            level = jnp.concatenate([p_hist, p], axis=0)
            sums = []
            for g, w in enumerate(POOL_WINDOWS):
                level = level + pltpu.roll(level, w // 2, axis=0)
                sums.append(level[:, 0:D_GROUP])
                if g + 1 < N_GROUPS:
                    level = level[:, D_GROUP:]
            win = jnp.concatenate(sums, axis=1)[POOL_HIST:]
            scaled = jnp.concatenate([win[:POOL_HIST] * first, win[POOL_HIST:] * invw_ref[...]], axis=0)
            y = _pool_groups((scaled - p).astype(BF16), pw_ref, pscale_ref)

            vb = v.astype(BF16)
            n_chunks = st // CHUNK
            per_head = []
            for hh in range(N_HEADS):
                rhs = jnp.concatenate(
                    [vb[c * CHUNK:(c + 1) * CHUNK, hh * D_HEAD:(hh + 1) * D_HEAD] for c in range(n_chunks)],
                    axis=1)
                res = _dot(wsm_ref[hh], rhs)
                bias = bias_ref[:, hh * D_HEAD:(hh + 1) * D_HEAD]
                per_head.append([res[:, c * D_HEAD:(c + 1) * D_HEAD] + bias for c in range(n_chunks)])
            mix = jnp.concatenate(
                [jnp.concatenate([per_head[hh][c] for hh in range(N_HEADS)], axis=1) for c in range(n_chunks)],
                axis=0)
            return _mixer_merge(x, ga, gb, y, mix, u, wpo_ref, wso_ref, wo_ref), p[st - POOL_HIST:]

        invw_rows = jnp.broadcast_to(invw_ref[...], tbl_ref.shape)
        p_hist = carry_p[...]
        x1s = []
        for s in range(n_sub):
            first = jnp.where(first_tile, tbl_ref[...], invw_rows) if s == 0 else invw_rows
            x1, p_hist = mixer(xp_ref[0, s * st:(s + 1) * st], p_hist, first)
            x1s.append(x1)
        carry_p[...] = p_hist
        npoolp_ref[0, 0] = p_hist[POOL_HIST - POOL_PAD:]

        a_hist = [carry_a[:, j * FF_CHUNK:(j + 1) * FF_CHUNK] for j in range(N_FF_CHUNKS)]
        for s in range(n_sub):
            def shifted(j, a, a_hist=a_hist):
                ext = jnp.concatenate([a_hist[j], a], axis=0)
                return pltpu.roll(ext, 1, axis=0)[SUBLANES:], pltpu.roll(ext, 2, axis=0)[SUBLANES:]

            y_out, a_parts = conv_ffn(x1s[s], shifted)
            a_hist = [a[st - SUBLANES:] for a in a_parts]
            yp_ref[0, s * st:(s + 1) * st] = y_out
        for j in range(N_FF_CHUNKS):
            carry_a[:, j * FF_CHUNK:(j + 1) * FF_CHUNK] = a_hist[j]
            nconvp_ref[0, 0, :, j * FF_CHUNK:(j + 1) * FF_CHUNK] = a_hist[j][SUBLANES - (CONV_K - 1):]

    @pl.when(step >= n_prompt_steps)
    def _sample():
        rt = SAMPLE_TILE
        nb = rt // dec_seq
        x = xs_ref[...]
        h = _rms(x, g1_ref[...]).astype(BF16)
        p, u, v, ga, gb = _input_proj(h, w_in_ref, sgug_ref)
        vs_ref[...] = v

        stp = stp_ref[0]
        old = jnp.concatenate([stp[:, i, :] for i in range(POOL_PAD)]
                              + [jnp.zeros((nb, D_POOL), F32)], axis=0)
        p_hi, p_lo = _split_bf16(p)
        s_hi, s_lo = _split_bf16(old)
        means = []
        for g in range(N_GROUPS):
            sl = slice(g * D_GROUP, (g + 1) * D_GROUP)
            new2 = jnp.concatenate([p_hi[:, sl], p_lo[:, sl]], axis=1)
            old2 = jnp.concatenate([s_hi[:, sl], s_lo[:, sl]], axis=1)
            r = _dot(pool_a_ref[g], new2) + _dot(pool_b_ref[g], old2)
            means.append(r[:, :D_GROUP] + r[:, D_GROUP:])
        y = _pool_groups((jnp.concatenate(means, axis=1) - p).astype(BF16), pw_ref, pscale_ref)
        p3 = p.reshape(nb, dec_seq, D_POOL)
        for i in range(POOL_PAD):
            k = i + dec_seq
            npools_ref[0, :, i, :] = stp[:, k, :] if k < POOL_PAD else p3[:, k - POOL_PAD, :]

        vb = v.astype(BF16)
        row = lax.broadcasted_iota(jnp.int32, (rt, rt), 0)
        col = lax.broadcasted_iota(jnp.int32, (rt, rt), 1)
        keep = ((row // dec_seq) == (col // dec_seq)) & (row >= col)
        mixes = []
        for hh in range(N_HEADS):
            w_rows = jnp.tile(sguw_ref[0, hh, 0:dec_seq, :], (nb, 1)).astype(BF16)
            w_s = jnp.where(keep, _dot(w_rows, onehot_ref[...]), 0.0).astype(BF16)
            bias = jnp.tile(bias_ref[0:dec_seq, hh * D_HEAD:(hh + 1) * D_HEAD], (nb, 1))
            mixes.append(_dot(w_s, vb[:, hh * D_HEAD:(hh + 1) * D_HEAD]) + bias)
        mix = jnp.concatenate(mixes, axis=1)

        x1 = _mixer_merge(x, ga, gb, y, mix, u, wpo_ref, wso_ref, wo_ref)

        tok = lax.broadcasted_iota(jnp.int32, (rt, FF_CHUNK), 0) % dec_seq

        def shifted(j, a):
            lo, hi = j * FF_CHUNK, (j + 1) * FF_CHUNK
            per_row = lambda r: jnp.broadcast_to(r[:, None, :], (nb, dec_seq, FF_CHUNK)).reshape(rt, FF_CHUNK)
            back2 = per_row(stc_ref[:, lo:hi])
            back1 = per_row(stc_ref[:, D_FF + lo:D_FF + hi])
            s1 = jnp.where(tok == 0, back1, pltpu.roll(a, 1, axis=0))
            s2 = jnp.where(tok == 0, back2, jnp.where(tok == 1, back1, pltpu.roll(a, 2, axis=0)))
            return s1, s2

        y_out, a_parts = conv_ffn(x1, shifted)
        ys_ref[...] = y_out
        for j, a in enumerate(a_parts):
            a3 = a.reshape(nb, dec_seq, FF_CHUNK)
            for r in range(CONV_K - 1):
                lo = r * D_FF + j * FF_CHUNK
                nconvs_ref[:, lo:lo + FF_CHUNK] = a3[:, dec_seq - (CONV_K - 1) + r, :]


def _pool_matrices(dec_seq, n_batch):
    a = np.zeros((N_GROUPS, n_batch * dec_seq, n_batch * dec_seq), np.float32)
    b = np.zeros((N_GROUPS, n_batch * dec_seq, POOL_HIST * n_batch), np.float32)
    for g, w in enumerate(POOL_WINDOWS):
        for bb in range(n_batch):
            for t in range(dec_seq):
                for k in range(w):
                    i = POOL_PAD + t - k
                    if i >= POOL_PAD:
                        a[g, bb * dec_seq + t, bb * dec_seq + i - POOL_PAD] = 1.0 / w
                    else:
                        b[g, bb * dec_seq + t, i * n_batch + bb] = 1.0 / w
    return jnp.asarray(a, BF16), jnp.asarray(b, BF16)


def _first_rows_table():
    t = np.arange(POOL_HIST, dtype=np.float32)[:, None]
    w = np.repeat(np.asarray(POOL_WINDOWS, np.float32), D_GROUP)[None, :]
    return jnp.asarray(1.0 / np.minimum(w, t + 1.0), F32), jnp.asarray(1.0 / w, F32)


def _whole(shape):
    zeros = (0,) * len(shape)
    return pl.BlockSpec(shape, lambda i: zeros, pipeline_mode=pl.Buffered(1))


def kernel(x_prompt, x_sample, state_pool, state_ffn_conv, norm1_g, w_in, pool_w, pool_scale,
           w_pool_out, sgu_norm_g, sgu_w, sgu_b, w_sgu_out, w_o, norm2_g, ffn_w_up, ffn_w_gate,
           ffn_conv_w, ffn_conv_b, ffn_w_down, final_norm_g):
    depth = norm1_g.shape[0]
    assert depth == 1
    batch, seq, _ = x_prompt.shape
    dec_batch, dec_seq, _ = x_sample.shape
    assert seq % PROMPT_TILE == 0 and PROMPT_SUBTILE % CHUNK == 0 and PROMPT_SUBTILE >= 2 * POOL_HIST
    assert SAMPLE_TILE % dec_seq == 0 and (dec_batch * dec_seq) % SAMPLE_TILE == 0
    assert CONV_K - 1 <= dec_seq <= CHUNK and dec_seq % SUBLANES == 0
    assert sgu_w.shape[-1] == CHUNK and state_pool.shape[2] == POOL_PAD

    tiles_per_seq = seq // PROMPT_TILE
    n_p = batch * tiles_per_seq
    rows = dec_batch * dec_seq
    n_s = rows // SAMPLE_TILE
    tile_batch = SAMPLE_TILE // dec_seq

    tbl, invw = _first_rows_table()
    pool_a, pool_b = _pool_matrices(dec_seq, tile_batch)
    onehot = np.zeros((CHUNK, SAMPLE_TILE), np.float32)
    onehot[np.arange(SAMPLE_TILE) % dec_seq, np.arange(SAMPLE_TILE)] = 1.0
    onehot = jnp.asarray(onehot, BF16)

    p_tile = lambda i: jnp.minimum(i, n_p - 1)
    s_tile = lambda i: jnp.maximum(i - n_p, 0)
    s_rows = lambda n: pl.BlockSpec((SAMPLE_TILE, n), lambda i: (s_tile(i), 0), pipeline_mode=pl.Buffered(1))
    s_batches = lambda n: pl.BlockSpec((tile_batch, n), lambda i: (s_tile(i), 0), pipeline_mode=pl.Buffered(1))
    hbm = pl.BlockSpec(memory_space=pl.ANY)

    s_pool = pl.BlockSpec((1, tile_batch, POOL_PAD, D_POOL), lambda i: (0, s_tile(i), 0, 0),
                          pipeline_mode=pl.Buffered(1))
    tiled = [x_prompt, x_sample.reshape(rows, D_MODEL), state_pool,
             state_ffn_conv.reshape(dec_batch, (CONV_K - 1) * D_FF)]
    tiled_specs = [
        pl.BlockSpec((1, PROMPT_TILE, D_MODEL), lambda i: (p_tile(i) // tiles_per_seq, p_tile(i) % tiles_per_seq, 0)),
        s_rows(D_MODEL), s_pool, s_batches((CONV_K - 1) * D_FF)]
    small = [norm1_g, pool_scale, sgu_norm_g, sgu_b, norm2_g, ffn_conv_w, ffn_conv_b,
             final_norm_g.reshape(1, D_MODEL), sgu_w, pool_w, invw, tbl, pool_a, pool_b, onehot]
    big = [w_in, w_pool_out, w_sgu_out, w_o, ffn_w_up, ffn_w_gate, ffn_w_down]

    out_shape = [
        jax.ShapeDtypeStruct((batch, seq, D_MODEL), F32),
        jax.ShapeDtypeStruct((rows, D_MODEL), F32),
        jax.ShapeDtypeStruct((1, batch, POOL_PAD, D_POOL), F32),
        jax.ShapeDtypeStruct((1, batch, CONV_K - 1, D_FF), F32),
        jax.ShapeDtypeStruct((1, dec_batch, POOL_PAD, D_POOL), F32),
        jax.ShapeDtypeStruct((dec_batch, (CONV_K - 1) * D_FF), F32),
        jax.ShapeDtypeStruct((rows, D_SGU), F32)]
    out_specs = [
        pl.BlockSpec((1, PROMPT_TILE, D_MODEL), lambda i: (p_tile(i) // tiles_per_seq, p_tile(i) % tiles_per_seq, 0)),
        s_rows(D_MODEL),
        pl.BlockSpec((1, 1, POOL_PAD, D_POOL), lambda i: (0, p_tile(i) // tiles_per_seq, 0, 0)),
        pl.BlockSpec((1, 1, CONV_K - 1, D_FF), lambda i: (0, p_tile(i) // tiles_per_seq, 0, 0)),
        s_pool, s_batches((CONV_K - 1) * D_FF), s_rows(D_SGU)]
    scratch = [
        pltpu.VMEM((D_MODEL, D_IN), BF16), pltpu.VMEM((D_POOL, D_MODEL), BF16), pltpu.VMEM((D_SGU, D_MODEL), BF16),
        pltpu.VMEM((D_MODEL, D_MODEL), BF16), pltpu.VMEM((D_MODEL, D_FF), BF16), pltpu.VMEM((D_MODEL, D_FF), BF16),
        pltpu.VMEM((D_FF, D_MODEL), BF16),
        pltpu.VMEM((N_GROUPS // 2, 2 * D_GROUP, 2 * D_GROUP), BF16),
        pltpu.VMEM((N_HEADS, CHUNK, CHUNK), BF16),
        pltpu.VMEM((CHUNK, D_SGU), F32),
        pltpu.VMEM((POOL_HIST, D_POOL), F32), pltpu.VMEM((SUBLANES, D_FF), F32)]

    outs = pl.pallas_call(
        functools.partial(_layer_kernel, n_prompt_steps=n_p, tiles_per_seq=tiles_per_seq, dec_seq=dec_seq),
        grid=(n_p + n_s,),
        in_specs=tiled_specs + [_whole(a.shape) for a in small] + [hbm] * len(big),
        out_specs=out_specs,
        out_shape=out_shape,
        scratch_shapes=scratch,
        compiler_params=pltpu.CompilerParams(dimension_semantics=("arbitrary",),
                                             vmem_limit_bytes=VMEM_LIMIT_BYTES),
        name="layer_step",
    )(*tiled, *small, *big)
    y_prompt, y_s, npool_p, nconv_p, npool_s, nconv_s, v_s = outs
    return (y_prompt, y_s.reshape(dec_batch, dec_seq, D_MODEL),
            npool_p, npool_s,
            nconv_p, nconv_s.reshape(1, dec_batch, CONV_K - 1, D_FF),
            v_s.reshape(1, dec_batch, dec_seq, D_SGU))
```

```python
import functools

import numpy as np
import jax
import jax.numpy as jnp
from jax import lax
from jax.experimental import pallas as pl
from jax.experimental.pallas import tpu as pltpu

D_MODEL = 1024
POOL_WINDOWS = (2, 4, 8, 16)
N_GROUPS = 4
D_GROUP = 128
D_POOL = N_GROUPS * D_GROUP
POOL_PAD = max(POOL_WINDOWS) - 1
POOL_HIST = POOL_PAD + 1
CHUNK = 128
N_HEADS = 4
D_SGU = 512
D_HEAD = D_SGU // N_HEADS
D_IN = D_POOL + 2 * D_SGU + 2 * D_MODEL
D_FF = 2816
CONV_K = 3
EPS = 1e-6

SUBLANES = 8
LANES = 128
FF_CHUNK = 256
N_FF_CHUNKS = D_FF // FF_CHUNK
PROMPT_SUBTILE = 256
PROMPT_TILE = 2 * PROMPT_SUBTILE
SAMPLE_TILE = 256
WIDE_ROWS = 128
NARROW_ROWS = 256
LOAD_SLOTS = 4
CAST_ROWS = 16
VMEM_LIMIT_BYTES = 60 * 1024 * 1024

_GELU_C = 0.7978845608028654
_GELU_C3 = _GELU_C * 0.044715

BF16 = jnp.bfloat16
F32 = jnp.float32


def _dot(a, b):
    return jnp.dot(a, b, preferred_element_type=F32)


def _rms(x, g):
    ms = jnp.mean(x * x, axis=-1, keepdims=True)
    return (x * lax.rsqrt(ms + EPS)) * g


def _gelu(x):
    u = x * (_GELU_C + _GELU_C3 * (x * x))
    return x * (0.5 + 0.5 * jnp.tanh(u))


def _sigmoid(x):
    return 0.5 * jnp.tanh(0.5 * x) + 0.5


def _split_bf16(x):
    hi = x.astype(BF16)
    lo = (x - hi.astype(F32)).astype(BF16)
    return hi, lo


def _input_proj(h, w_in_ref, sgug_ref):
    p = _dot(h, w_in_ref[:, 0:D_POOL])
    u = _gelu(_dot(h, w_in_ref[:, D_POOL:D_POOL + D_SGU]))
    v = _rms(_gelu(_dot(h, w_in_ref[:, D_POOL + D_SGU:D_POOL + 2 * D_SGU])), sgug_ref[...])
    g_lo = D_POOL + 2 * D_SGU
    ga = _sigmoid(_dot(h, w_in_ref[:, g_lo:g_lo + D_MODEL]))
    gb = _sigmoid(_dot(h, w_in_ref[:, g_lo + D_MODEL:g_lo + 2 * D_MODEL]))
    return p, u, v, ga, gb


def _pool_groups(d, pw_ref, pscale_ref):
    y01 = _dot(d[:, 0:2 * D_GROUP], pw_ref[0])
    y23 = _dot(d[:, 2 * D_GROUP:], pw_ref[1])
    return (jnp.concatenate([y01, y23], axis=1) * pscale_ref[...]).astype(BF16)


def _mixer_merge(x, ga, gb, y, mix, u, wpo_ref, wso_ref, wo_ref):
    a_out = _dot(y, wpo_ref[...])
    b_out = _dot((u * mix).astype(BF16), wso_ref[...])
    m = (ga * a_out + gb * b_out).astype(BF16)
    return x + _dot(m, wo_ref[...])


def _load_weight_group(weights, chunk_rows, max_cols):
    chunks = [(hbm, vmem, r0, cols)
              for hbm, vmem, rows, cols in weights for r0 in range(0, rows, chunk_rows)]
    lookahead = LOAD_SLOTS - 1

    def run(stage, sem):
        def copy(g):
            hbm, _, r0, cols = chunks[g]
            slot = g % LOAD_SLOTS
            return pltpu.make_async_copy(hbm.at[0, r0:r0 + chunk_rows, :],
                                         stage.at[slot, :, 0:cols], sem.at[slot])

        for g in range(min(lookahead, len(chunks))):
            copy(g).start()
        for g, (_, vmem, r0, cols) in enumerate(chunks):
            if g + lookahead < len(chunks):
                copy(g + lookahead).start()
            copy(g).wait()
            slot = g % LOAD_SLOTS

            def cast(r, carry, vmem=vmem, r0=r0, cols=cols, slot=slot):
                rr = pl.multiple_of(r * CAST_ROWS, CAST_ROWS)
                vmem[pl.ds(r0 + rr, CAST_ROWS), :] = stage[slot, pl.ds(rr, CAST_ROWS), 0:cols].astype(BF16)
                return carry

            lax.fori_loop(0, chunk_rows // CAST_ROWS, cast, 0)

    pl.run_scoped(run, pltpu.VMEM((LOAD_SLOTS, chunk_rows, max_cols), F32),
                  pltpu.SemaphoreType.DMA((LOAD_SLOTS,)))


def _layer_kernel(
        xp_ref, xs_ref, stp_ref, stc_ref,
        g1_ref, pscale_ref, sgug_ref, sgub_ref, g2_ref, cw_ref, cb_ref, gf_ref, sguw_ref, poolw_ref,
        invw_ref, tbl_ref, pool_a_ref, pool_b_ref, onehot_ref,
        w_in_hbm, wpo_hbm, wso_hbm, wo_hbm, wup_hbm, wgate_hbm, wdown_hbm,
        yp_ref, ys_ref, npoolp_ref, nconvp_ref, npools_ref, nconvs_ref, vs_ref,
        w_in_ref, wpo_ref, wso_ref, wo_ref, wup_ref, wgate_ref, wdown_ref, pw_ref, wsm_ref, bias_ref,
        carry_p, carry_a,
        *, n_prompt_steps, tiles_per_seq, dec_seq):
    step = pl.program_id(0)

    @pl.when(step == 0)
    def _prepare():
        _load_weight_group(
            [(w_in_hbm, w_in_ref, D_MODEL, D_IN), (wup_hbm, wup_ref, D_MODEL, D_FF),
             (wgate_hbm, wgate_ref, D_MODEL, D_FF)], WIDE_ROWS, D_IN)
        _load_weight_group(
            [(wdown_hbm, wdown_ref, D_FF, D_MODEL), (wo_hbm, wo_ref, D_MODEL, D_MODEL),
             (wpo_hbm, wpo_ref, D_POOL, D_MODEL), (wso_hbm, wso_ref, D_SGU, D_MODEL)], NARROW_ROWS, D_MODEL)
        pw_ref[...] = jnp.zeros_like(pw_ref)
        for g in range(N_GROUPS):
            lo = (g % 2) * D_GROUP
            pw_ref[g // 2, lo:lo + D_GROUP, lo:lo + D_GROUP] = poolw_ref[0, g].astype(BF16)
        row = lax.broadcasted_iota(jnp.int32, (CHUNK, CHUNK), 0)
        col = lax.broadcasted_iota(jnp.int32, (CHUNK, CHUNK), 1)
        for hh in range(N_HEADS):
            wsm_ref[hh] = jnp.where(row >= col, sguw_ref[0, hh], 0.0).astype(BF16)
            b_row = sgub_ref[0, hh:hh + 1, :]
            bias_ref[:, hh * D_HEAD:(hh + 1) * D_HEAD] = jnp.broadcast_to(b_row, (D_HEAD, CHUNK)).T

    def conv_ffn(x1, shifted):
        h2 = _rms(x1, g2_ref[...]).astype(BF16)
        f_parts, a_parts = [], []
        for j in range(N_FF_CHUNKS):
            lo, hi = j * FF_CHUNK, (j + 1) * FF_CHUNK
            a = _dot(h2, wup_ref[:, lo:hi])
            s1, s2 = shifted(j, a)
            c = s2 * cw_ref[0, :, lo:hi] + s1 * cw_ref[1, :, lo:hi] + a * cw_ref[2, :, lo:hi]
            a_parts.append(a)
            f_parts.append((_gelu(c + cb_ref[:, lo:hi]) * _dot(h2, wgate_ref[:, lo:hi])).astype(BF16))
        x2 = x1 + _dot(jnp.concatenate(f_parts, axis=1), wdown_ref[...])
        return _rms(x2, gf_ref[...]), a_parts

    @pl.when(step < n_prompt_steps)
    def _prompt():
        st = PROMPT_SUBTILE
        n_sub = PROMPT_TILE // st
        first_tile = (step % tiles_per_seq) == 0

        @pl.when(first_tile)
        def _():
            carry_p[...] = jnp.zeros_like(carry_p)
            carry_a[...] = jnp.zeros_like(carry_a)

        def mixer(x, p_hist, first):
            h = _rms(x, g1_ref[...]).astype(BF16)
            p, u, v, ga, gb = _input_proj(h, w_in_ref, sgug_ref)

            level = jnp.concatenate([p_hist, p], axis=0)
            sums = []
            for g, w in enumerate(POOL_WINDOWS):
                level = level + pltpu.roll(level, w // 2, axis=0)
                sums.append(level[:, 0:D_GROUP])
                if g + 1 < N_GROUPS:
                    level = level[:, D_GROUP:]
            win = jnp.concatenate(sums, axis=1)[POOL_HIST:]
            scaled = jnp.concatenate([win[:POOL_HIST] * first, win[POOL_HIST:] * invw_ref[...]], axis=0)
            y = _pool_groups((scaled - p).astype(BF16), pw_ref, pscale_ref)

            vb = v.astype(BF16)
            n_chunks = st // CHUNK
            per_head = []
            for hh in range(N_HEADS):
                rhs = jnp.concatenate(
                    [vb[c * CHUNK:(c + 1) * CHUNK, hh * D_HEAD:(hh + 1) * D_HEAD] for c in range(n_chunks)],
                    axis=1)
                res = _dot(wsm_ref[hh], rhs)
                bias = bias_ref[:, hh * D_HEAD:(hh + 1) * D_HEAD]
                per_head.append([res[:, c * D_HEAD:(c + 1) * D_HEAD] + bias for c in range(n_chunks)])
            mix = jnp.concatenate(
                [jnp.concatenate([per_head[hh][c] for hh in range(N_HEADS)], axis=1) for c in range(n_chunks)],
                axis=0)
            return _mixer_merge(x, ga, gb, y, mix, u, wpo_ref, wso_ref, wo_ref), p[st - POOL_HIST:]

        invw_rows = jnp.broadcast_to(invw_ref[...], tbl_ref.shape)
        p_hist = carry_p[...]
        x1s = []
        for s in range(n_sub):
            first = jnp.where(first_tile, tbl_ref[...], invw_rows) if s == 0 else invw_rows
            x1, p_hist = mixer(xp_ref[0, s * st:(s + 1) * st], p_hist, first)
            x1s.append(x1)
        carry_p[...] = p_hist
        npoolp_ref[:, pl.ds(step // tiles_per_seq, 1), :] = p_hist[POOL_HIST - POOL_PAD:][:, None, :]

        a_hist = [carry_a[:, j * FF_CHUNK:(j + 1) * FF_CHUNK] for j in range(N_FF_CHUNKS)]
        for s in range(n_sub):
            def shifted(j, a, a_hist=a_hist):
                ext = jnp.concatenate([a_hist[j], a], axis=0)
                return pltpu.roll(ext, 1, axis=0)[SUBLANES:], pltpu.roll(ext, 2, axis=0)[SUBLANES:]

            y_out, a_parts = conv_ffn(x1s[s], shifted)
            a_hist = [a[st - SUBLANES:] for a in a_parts]
            yp_ref[0, s * st:(s + 1) * st] = y_out
        for j in range(N_FF_CHUNKS):
            carry_a[:, j * FF_CHUNK:(j + 1) * FF_CHUNK] = a_hist[j]
            nconvp_ref[0, 0, :, j * FF_CHUNK:(j + 1) * FF_CHUNK] = a_hist[j][SUBLANES - (CONV_K - 1):]

    @pl.when(step >= n_prompt_steps)
    def _sample():
        rt = SAMPLE_TILE
        nb = rt // dec_seq
        x = xs_ref[...]
        h = _rms(x, g1_ref[...]).astype(BF16)
        p, u, v, ga, gb = _input_proj(h, w_in_ref, sgug_ref)
        vs_ref[...] = v

        stp = stp_ref[...]
        old = jnp.concatenate([stp.reshape(POOL_PAD * nb, D_POOL), jnp.zeros((nb, D_POOL), F32)],
                              axis=0)
        p_hi, p_lo = _split_bf16(p)
        s_hi, s_lo = _split_bf16(old)
        means = []
        for g in range(N_GROUPS):
            sl = slice(g * D_GROUP, (g + 1) * D_GROUP)
            new2 = jnp.concatenate([p_hi[:, sl], p_lo[:, sl]], axis=1)
            old2 = jnp.concatenate([s_hi[:, sl], s_lo[:, sl]], axis=1)
            r = _dot(pool_a_ref[g], new2) + _dot(pool_b_ref[g], old2)
            means.append(r[:, :D_GROUP] + r[:, D_GROUP:])
        y = _pool_groups((jnp.concatenate(means, axis=1) - p).astype(BF16), pw_ref, pscale_ref)
        p3 = p.reshape(nb, dec_seq, D_POOL)
        for i in range(POOL_PAD):
            k = i + dec_seq
            npools_ref[i] = stp[k] if k < POOL_PAD else p3[:, k - POOL_PAD, :]

        vb = v.astype(BF16)
        row = lax.broadcasted_iota(jnp.int32, (rt, rt), 0)
        col = lax.broadcasted_iota(jnp.int32, (rt, rt), 1)
        keep = ((row // dec_seq) == (col // dec_seq)) & (row >= col)
        mixes = []
        for hh in range(N_HEADS):
            w_rows = jnp.tile(sguw_ref[0, hh, 0:dec_seq, :], (nb, 1)).astype(BF16)
            w_s = jnp.where(keep, _dot(w_rows, onehot_ref[...]), 0.0).astype(BF16)
            bias = jnp.tile(bias_ref[0:dec_seq, hh * D_HEAD:(hh + 1) * D_HEAD], (nb, 1))
            mixes.append(_dot(w_s, vb[:, hh * D_HEAD:(hh + 1) * D_HEAD]) + bias)
        mix = jnp.concatenate(mixes, axis=1)

        x1 = _mixer_merge(x, ga, gb, y, mix, u, wpo_ref, wso_ref, wo_ref)

        tok = lax.broadcasted_iota(jnp.int32, (nb, dec_seq, FF_CHUNK), 1)

        def shifted(j, a):
            lo, hi = j * FF_CHUNK, (j + 1) * FF_CHUNK
            a3 = a.reshape(nb, dec_seq, FF_CHUNK)
            back = stc_ref[0, :, :, lo:hi]
            back2, back1 = back[:, 0:1, :], back[:, 1:2, :]
            s1 = jnp.where(tok == 0, back1, pltpu.roll(a3, 1, axis=1))
            s2 = jnp.where(tok == 0, back2, jnp.where(tok == 1, back1, pltpu.roll(a3, 2, axis=1)))
            return s1.reshape(rt, FF_CHUNK), s2.reshape(rt, FF_CHUNK)

        y_out, a_parts = conv_ffn(x1, shifted)
        ys_ref[...] = y_out
        for j, a in enumerate(a_parts):
            a3 = a.reshape(nb, dec_seq, FF_CHUNK)
            nconvs_ref[0, :, :, j * FF_CHUNK:(j + 1) * FF_CHUNK] = a3[:, dec_seq - (CONV_K - 1):, :]


def _pool_matrices(dec_seq, n_batch):
    a = np.zeros((N_GROUPS, n_batch * dec_seq, n_batch * dec_seq), np.float32)
    b = np.zeros((N_GROUPS, n_batch * dec_seq, POOL_HIST * n_batch), np.float32)
    for g, w in enumerate(POOL_WINDOWS):
        for bb in range(n_batch):
            for t in range(dec_seq):
                for k in range(w):
                    i = POOL_PAD + t - k
                    if i >= POOL_PAD:
                        a[g, bb * dec_seq + t, bb * dec_seq + i - POOL_PAD] = 1.0 / w
                    else:
                        b[g, bb * dec_seq + t, i * n_batch + bb] = 1.0 / w
    return jnp.asarray(a, BF16), jnp.asarray(b, BF16)


def _first_rows_table():
    t = np.arange(POOL_HIST, dtype=np.float32)[:, None]
    w = np.repeat(np.asarray(POOL_WINDOWS, np.float32), D_GROUP)[None, :]
    return jnp.asarray(1.0 / np.minimum(w, t + 1.0), F32), jnp.asarray(1.0 / w, F32)


def _whole(shape):
    zeros = (0,) * len(shape)
    return pl.BlockSpec(shape, lambda i: zeros, pipeline_mode=pl.Buffered(1))


def kernel(x_prompt, x_sample, state_pool, state_ffn_conv, norm1_g, w_in, pool_w, pool_scale,
           w_pool_out, sgu_norm_g, sgu_w, sgu_b, w_sgu_out, w_o, norm2_g, ffn_w_up, ffn_w_gate,
           ffn_conv_w, ffn_conv_b, ffn_w_down, final_norm_g):
    depth = norm1_g.shape[0]
    assert depth == 1
    batch, seq, _ = x_prompt.shape
    dec_batch, dec_seq, _ = x_sample.shape
    assert seq % PROMPT_TILE == 0 and PROMPT_SUBTILE % CHUNK == 0 and PROMPT_SUBTILE >= 2 * POOL_HIST
    assert SAMPLE_TILE % dec_seq == 0 and (dec_batch * dec_seq) % SAMPLE_TILE == 0
    assert CONV_K - 1 <= dec_seq <= CHUNK and dec_seq % SUBLANES == 0
    assert sgu_w.shape[-1] == CHUNK and state_pool.shape[2] == POOL_PAD

    tiles_per_seq = seq // PROMPT_TILE
    n_p = batch * tiles_per_seq
    rows = dec_batch * dec_seq
    n_s = rows // SAMPLE_TILE
    tile_batch = SAMPLE_TILE // dec_seq

    tbl, invw = _first_rows_table()
    pool_a, pool_b = _pool_matrices(dec_seq, tile_batch)
    onehot = np.zeros((CHUNK, SAMPLE_TILE), np.float32)
    onehot[np.arange(SAMPLE_TILE) % dec_seq, np.arange(SAMPLE_TILE)] = 1.0
    onehot = jnp.asarray(onehot, BF16)

    p_tile = lambda i: jnp.minimum(i, n_p - 1)
    s_tile = lambda i: jnp.maximum(i - n_p, 0)
    s_rows = lambda n: pl.BlockSpec((SAMPLE_TILE, n), lambda i: (s_tile(i), 0), pipeline_mode=pl.Buffered(1))
    hbm = pl.BlockSpec(memory_space=pl.ANY)

    s_pool = pl.BlockSpec((POOL_PAD, tile_batch, D_POOL), lambda i: (0, s_tile(i), 0),
                          pipeline_mode=pl.Buffered(1))
    s_conv = pl.BlockSpec((1, tile_batch, CONV_K - 1, D_FF), lambda i: (0, s_tile(i), 0, 0),
                          pipeline_mode=pl.Buffered(1))
    tiled = [x_prompt, x_sample.reshape(rows, D_MODEL), jnp.transpose(state_pool[0], (1, 0, 2)),
             state_ffn_conv]
    tiled_specs = [
        pl.BlockSpec((1, PROMPT_TILE, D_MODEL), lambda i: (p_tile(i) // tiles_per_seq, p_tile(i) % tiles_per_seq, 0)),
        s_rows(D_MODEL), s_pool, s_conv]
    small = [norm1_g, pool_scale, sgu_norm_g, sgu_b, norm2_g, jnp.transpose(ffn_conv_w, (1, 0, 2)), ffn_conv_b,
             final_norm_g.reshape(1, D_MODEL), sgu_w, pool_w, invw, tbl, pool_a, pool_b, onehot]
    big = [w_in, w_pool_out, w_sgu_out, w_o, ffn_w_up, ffn_w_gate, ffn_w_down]

    out_shape = [
        jax.ShapeDtypeStruct((batch, seq, D_MODEL), F32),
        jax.ShapeDtypeStruct((rows, D_MODEL), F32),
        jax.ShapeDtypeStruct((POOL_PAD, batch, D_POOL), F32),
        jax.ShapeDtypeStruct((1, batch, CONV_K - 1, D_FF), F32),
        jax.ShapeDtypeStruct((POOL_PAD, dec_batch, D_POOL), F32),
        jax.ShapeDtypeStruct((1, dec_batch, CONV_K - 1, D_FF), F32),
        jax.ShapeDtypeStruct((rows, D_SGU), F32)]
    out_specs = [
        pl.BlockSpec((1, PROMPT_TILE, D_MODEL), lambda i: (p_tile(i) // tiles_per_seq, p_tile(i) % tiles_per_seq, 0)),
        s_rows(D_MODEL),
        pl.BlockSpec((POOL_PAD, batch, D_POOL), lambda i: (0, 0, 0)),
        pl.BlockSpec((1, 1, CONV_K - 1, D_FF), lambda i: (0, p_tile(i) // tiles_per_seq, 0, 0)),
        s_pool, s_conv, s_rows(D_SGU)]
    scratch = [
        pltpu.VMEM((D_MODEL, D_IN), BF16), pltpu.VMEM((D_POOL, D_MODEL), BF16), pltpu.VMEM((D_SGU, D_MODEL), BF16),
        pltpu.VMEM((D_MODEL, D_MODEL), BF16), pltpu.VMEM((D_MODEL, D_FF), BF16), pltpu.VMEM((D_MODEL, D_FF), BF16),
        pltpu.VMEM((D_FF, D_MODEL), BF16),
        pltpu.VMEM((N_GROUPS // 2, 2 * D_GROUP, 2 * D_GROUP), BF16),
        pltpu.VMEM((N_HEADS, CHUNK, CHUNK), BF16),
        pltpu.VMEM((CHUNK, D_SGU), F32),
        pltpu.VMEM((POOL_HIST, D_POOL), F32), pltpu.VMEM((SUBLANES, D_FF), F32)]

    outs = pl.pallas_call(
        functools.partial(_layer_kernel, n_prompt_steps=n_p, tiles_per_seq=tiles_per_seq, dec_seq=dec_seq),
        grid=(n_p + n_s,),
        in_specs=tiled_specs + [_whole(a.shape) for a in small] + [hbm] * len(big),
        out_specs=out_specs,
        out_shape=out_shape,
        scratch_shapes=scratch,
        compiler_params=pltpu.CompilerParams(dimension_semantics=("arbitrary",),
                                             vmem_limit_bytes=VMEM_LIMIT_BYTES),
        name="layer_step",
    )(*tiled, *small, *big)
    y_prompt, y_s, npool_p, nconv_p, npool_s, nconv_s, v_s = outs
    return (y_prompt, y_s.reshape(dec_batch, dec_seq, D_MODEL),
            jnp.transpose(npool_p, (1, 0, 2))[None], jnp.transpose(npool_s, (1, 0, 2))[None],
            nconv_p, nconv_s,
            v_s.reshape(1, dec_batch, dec_seq, D_SGU))
```

```python
import functools

import numpy as np
import jax
import jax.numpy as jnp
from jax import lax
from jax.experimental import pallas as pl
from jax.experimental.pallas import tpu as pltpu

D_MODEL = 1024
POOL_WINDOWS = (2, 4, 8, 16)
N_GROUPS = 4
D_GROUP = 128
D_POOL = N_GROUPS * D_GROUP
POOL_PAD = max(POOL_WINDOWS) - 1
POOL_HIST = POOL_PAD + 1
CHUNK = 128
N_HEADS = 4
D_SGU = 512
D_HEAD = D_SGU // N_HEADS
D_IN = D_POOL + 2 * D_SGU + 2 * D_MODEL
D_FF = 2816
CONV_K = 3
EPS = 1e-6

SUBLANES = 8
LANES = 128
FF_CHUNK = 256
N_FF_CHUNKS = D_FF // FF_CHUNK
PROMPT_SUBTILE = 256
PROMPT_TILE = 2 * PROMPT_SUBTILE
SAMPLE_TILE = 256
WIDE_ROWS = 128
NARROW_ROWS = 256
LOAD_SLOTS = 4
CAST_ROWS = 16
VMEM_LIMIT_BYTES = 60 * 1024 * 1024

_GELU_C = 0.7978845608028654
_GELU_C3 = _GELU_C * 0.044715

BF16 = jnp.bfloat16
F32 = jnp.float32


def _dot(a, b):
    return jnp.dot(a, b, preferred_element_type=F32)


def _rms(x, g):
    ms = jnp.mean(x * x, axis=-1, keepdims=True)
    return (x * lax.rsqrt(ms + EPS)) * g


def _gelu_of_twice(hx):
    u = hx * (2.0 * _GELU_C + (8.0 * _GELU_C3) * (hx * hx))
    return hx + hx * jnp.tanh(u)


def _sigmoid_of_twice(hx):
    return 0.5 * jnp.tanh(hx) + 0.5


def _split_bf16(x):
    hi = x.astype(BF16)
    lo = (x - hi.astype(F32)).astype(BF16)
    return hi, lo


def _input_proj(h, w_in_ref, sgug_ref):
    p = _dot(h, w_in_ref[:, 0:D_POOL])
    u = _gelu_of_twice(_dot(h, w_in_ref[:, D_POOL:D_POOL + D_SGU]))
    v = _rms(_gelu_of_twice(_dot(h, w_in_ref[:, D_POOL + D_SGU:D_POOL + 2 * D_SGU])), sgug_ref[...])
    g_lo = D_POOL + 2 * D_SGU
    ga = _sigmoid_of_twice(_dot(h, w_in_ref[:, g_lo:g_lo + D_MODEL]))
    gb = _sigmoid_of_twice(_dot(h, w_in_ref[:, g_lo + D_MODEL:g_lo + 2 * D_MODEL]))
    return p, u, v, ga, gb


def _pool_groups(d, pw_ref, pscale_ref):
    y01 = _dot(d[:, 0:2 * D_GROUP], pw_ref[0])
    y23 = _dot(d[:, 2 * D_GROUP:], pw_ref[1])
    return (jnp.concatenate([y01, y23], axis=1) * pscale_ref[...]).astype(BF16)


def _mixer_merge(x, ga, gb, y, mix, u, wpo_ref, wso_ref, wo_ref):
    a_out = _dot(y, wpo_ref[...])
    b_out = _dot((u * mix).astype(BF16), wso_ref[...])
    m = (ga * a_out + gb * b_out).astype(BF16)
    return x + _dot(m, wo_ref[...])


def _load_weight_group(weights, chunk_rows, max_cols):
    chunks = [(hbm, vmem, r0, cols, scale)
              for hbm, vmem, rows, cols, scale in weights for r0 in range(0, rows, chunk_rows)]
    lookahead = LOAD_SLOTS - 1

    def run(stage, sem):
        def copy(g):
            hbm, _, r0, cols, _ = chunks[g]
            slot = g % LOAD_SLOTS
            return pltpu.make_async_copy(hbm.at[0, r0:r0 + chunk_rows, :],
                                         stage.at[slot, :, 0:cols], sem.at[slot])

        for g in range(min(lookahead, len(chunks))):
            copy(g).start()
        for g, (_, vmem, r0, cols, scale) in enumerate(chunks):
            if g + lookahead < len(chunks):
                copy(g + lookahead).start()
            copy(g).wait()
            slot = g % LOAD_SLOTS

            def cast(r, carry, vmem=vmem, r0=r0, cols=cols, slot=slot, scale=scale):
                rr = pl.multiple_of(r * CAST_ROWS, CAST_ROWS)
                w = stage[slot, pl.ds(rr, CAST_ROWS), 0:cols]
                if scale is not None:
                    w = w * scale[...]
                vmem[pl.ds(r0 + rr, CAST_ROWS), :] = w.astype(BF16)
                return carry

            lax.fori_loop(0, chunk_rows // CAST_ROWS, cast, 0)

    pl.run_scoped(run, pltpu.VMEM((LOAD_SLOTS, chunk_rows, max_cols), F32),
                  pltpu.SemaphoreType.DMA((LOAD_SLOTS,)))


def _layer_kernel(
        xp_ref, xs_ref, stp_ref, stc_ref,
        g1_ref, pscale_ref, sgug_ref, sgub_ref, g2_ref, cw_ref, cb_ref, gf_ref, sguw_ref, poolw_ref,
        invw_ref, tbl_ref, pool_a_ref, pool_b_ref, onehot_ref, wscale_ref,
        w_in_hbm, wpo_hbm, wso_hbm, wo_hbm, wup_hbm, wgate_hbm, wdown_hbm,
        yp_ref, ys_ref, npoolp_ref, nconvp_ref, npools_ref, nconvs_ref, vs_ref,
        w_in_ref, wpo_ref, wso_ref, wo_ref, wup_ref, wgate_ref, wdown_ref, pw_ref, wsm_ref, bias_ref,
        carry_p, carry_a, hcw_ref, hcb_ref, wsb_ref,
        *, n_prompt_steps, tiles_per_seq, dec_seq):
    step = pl.program_id(0)

    @pl.when(step == 0)
    def _prepare():
        _load_weight_group(
            [(w_in_hbm, w_in_ref, D_MODEL, D_IN, wscale_ref), (wup_hbm, wup_ref, D_MODEL, D_FF, None),
             (wgate_hbm, wgate_ref, D_MODEL, D_FF, None)], WIDE_ROWS, D_IN)
        _load_weight_group(
            [(wdown_hbm, wdown_ref, D_FF, D_MODEL, None), (wo_hbm, wo_ref, D_MODEL, D_MODEL, None),
             (wpo_hbm, wpo_ref, D_POOL, D_MODEL, None), (wso_hbm, wso_ref, D_SGU, D_MODEL, None)],
            NARROW_ROWS, D_MODEL)
        hcw_ref[...] = 0.5 * cw_ref[...]
        hcb_ref[...] = 0.5 * cb_ref[...]
        pw_ref[...] = jnp.zeros_like(pw_ref)
        for g in range(N_GROUPS):
            lo = (g % 2) * D_GROUP
            pw_ref[g // 2, lo:lo + D_GROUP, lo:lo + D_GROUP] = poolw_ref[0, g].astype(BF16)
        row = lax.broadcasted_iota(jnp.int32, (CHUNK, CHUNK), 0)
        col = lax.broadcasted_iota(jnp.int32, (CHUNK, CHUNK), 1)
        for hh in range(N_HEADS):
            wsm_ref[hh] = jnp.where(row >= col, sguw_ref[0, hh], 0.0).astype(BF16)
            b_row = sgub_ref[0, hh:hh + 1, :]
            bias_ref[:, hh * D_HEAD:(hh + 1) * D_HEAD] = jnp.broadcast_to(b_row, (D_HEAD, CHUNK)).T
        rt = SAMPLE_TILE
        row = lax.broadcasted_iota(jnp.int32, (rt, rt), 0)
        col = lax.broadcasted_iota(jnp.int32, (rt, rt), 1)
        keep = ((row // dec_seq) == (col // dec_seq)) & (row >= col)
        for hh in range(N_HEADS):
            w_rows = jnp.tile(sguw_ref[0, hh, 0:dec_seq, :], (rt // dec_seq, 1)).astype(BF16)
            wsb_ref[hh] = jnp.where(keep, _dot(w_rows, onehot_ref[...]), 0.0).astype(BF16)

    def conv_ffn(x1, shifted):
        h2 = _rms(x1, g2_ref[...]).astype(BF16)
        f_parts, a_parts = [], []
        for j in range(N_FF_CHUNKS):
            lo, hi = j * FF_CHUNK, (j + 1) * FF_CHUNK
            a = _dot(h2, wup_ref[:, lo:hi])
            s1, s2 = shifted(j, a)
            half_c = (s2 * hcw_ref[0, :, lo:hi] + s1 * hcw_ref[1, :, lo:hi] + a * hcw_ref[2, :, lo:hi]
                      + hcb_ref[:, lo:hi])
            a_parts.append(a)
            f_parts.append((_gelu_of_twice(half_c) * _dot(h2, wgate_ref[:, lo:hi])).astype(BF16))
        x2 = x1 + _dot(jnp.concatenate(f_parts, axis=1), wdown_ref[...])
        return _rms(x2, gf_ref[...]), a_parts

    @pl.when(step < n_prompt_steps)
    def _prompt():
        st = PROMPT_SUBTILE
        n_sub = PROMPT_TILE // st
        first_tile = (step % tiles_per_seq) == 0

        @pl.when(first_tile)
        def _():
            carry_p[...] = jnp.zeros_like(carry_p)
            carry_a[...] = jnp.zeros_like(carry_a)

        def mixer(x, p_hist, first):
            h = _rms(x, g1_ref[...]).astype(BF16)
            p, u, v, ga, gb = _input_proj(h, w_in_ref, sgug_ref)

            level = jnp.concatenate([p_hist, p], axis=0)
            sums = []
            for g, w in enumerate(POOL_WINDOWS):
                level = level + pltpu.roll(level, w // 2, axis=0)
                sums.append(level[:, 0:D_GROUP])
                if g + 1 < N_GROUPS:
                    level = level[:, D_GROUP:]
            win = jnp.concatenate(sums, axis=1)[POOL_HIST:]
            scaled = jnp.concatenate([win[:POOL_HIST] * first, win[POOL_HIST:] * invw_ref[...]], axis=0)
            y = _pool_groups((scaled - p).astype(BF16), pw_ref, pscale_ref)

            vb = v.astype(BF16)
            n_chunks = st // CHUNK
            per_head = []
            for hh in range(N_HEADS):
                rhs = jnp.concatenate(
                    [vb[c * CHUNK:(c + 1) * CHUNK, hh * D_HEAD:(hh + 1) * D_HEAD] for c in range(n_chunks)],
                    axis=1)
                res = _dot(wsm_ref[hh], rhs)
                bias = bias_ref[:, hh * D_HEAD:(hh + 1) * D_HEAD]
                per_head.append([res[:, c * D_HEAD:(c + 1) * D_HEAD] + bias for c in range(n_chunks)])
            mix = jnp.concatenate(
                [jnp.concatenate([per_head[hh][c] for hh in range(N_HEADS)], axis=1) for c in range(n_chunks)],
                axis=0)
            return _mixer_merge(x, ga, gb, y, mix, u, wpo_ref, wso_ref, wo_ref), p[st - POOL_HIST:]

        invw_rows = jnp.broadcast_to(invw_ref[...], tbl_ref.shape)
        p_hist = carry_p[...]
        x1s = []
        for s in range(n_sub):
            first = jnp.where(first_tile, tbl_ref[...], invw_rows) if s == 0 else invw_rows
            x1, p_hist = mixer(xp_ref[0, s * st:(s + 1) * st], p_hist, first)
            x1s.append(x1)
        carry_p[...] = p_hist
        npoolp_ref[:, pl.ds(step // tiles_per_seq, 1), :] = p_hist[POOL_HIST - POOL_PAD:][:, None, :]

        a_hist = [carry_a[:, j * FF_CHUNK:(j + 1) * FF_CHUNK] for j in range(N_FF_CHUNKS)]
        for s in range(n_sub):
            def shifted(j, a, a_hist=a_hist):
                ext = jnp.concatenate([a_hist[j], a], axis=0)
                return pltpu.roll(ext, 1, axis=0)[SUBLANES:], pltpu.roll(ext, 2, axis=0)[SUBLANES:]

            y_out, a_parts = conv_ffn(x1s[s], shifted)
            a_hist = [a[st - SUBLANES:] for a in a_parts]
            yp_ref[0, s * st:(s + 1) * st] = y_out
        for j in range(N_FF_CHUNKS):
            carry_a[:, j * FF_CHUNK:(j + 1) * FF_CHUNK] = a_hist[j]
            nconvp_ref[0, 0, :, j * FF_CHUNK:(j + 1) * FF_CHUNK] = a_hist[j][SUBLANES - (CONV_K - 1):]

    @pl.when(step >= n_prompt_steps)
    def _sample():
        rt = SAMPLE_TILE
        nb = rt // dec_seq
        x = xs_ref[...]
        h = _rms(x, g1_ref[...]).astype(BF16)
        p, u, v, ga, gb = _input_proj(h, w_in_ref, sgug_ref)
        vs_ref[...] = v

        stp = stp_ref[...]
        old = jnp.concatenate([stp.reshape(POOL_PAD * nb, D_POOL), jnp.zeros((nb, D_POOL), F32)],
                              axis=0)
        p_hi, p_lo = _split_bf16(p)
        s_hi, s_lo = _split_bf16(old)
        means = []
        for g in range(N_GROUPS):
            sl = slice(g * D_GROUP, (g + 1) * D_GROUP)
            new2 = jnp.concatenate([p_hi[:, sl], p_lo[:, sl]], axis=1)
            old2 = jnp.concatenate([s_hi[:, sl], s_lo[:, sl]], axis=1)
            r = _dot(pool_a_ref[g], new2) + _dot(pool_b_ref[g], old2)
            means.append(r[:, :D_GROUP] + r[:, D_GROUP:])
        y = _pool_groups((jnp.concatenate(means, axis=1) - p).astype(BF16), pw_ref, pscale_ref)
        p3 = p.reshape(nb, dec_seq, D_POOL)
        for i in range(POOL_PAD):
            k = i + dec_seq
            npools_ref[i] = stp[k] if k < POOL_PAD else p3[:, k - POOL_PAD, :]

        vb = v.astype(BF16)
        mixes = []
        for hh in range(N_HEADS):
            bias = jnp.tile(bias_ref[0:dec_seq, hh * D_HEAD:(hh + 1) * D_HEAD], (nb, 1))
            mixes.append(_dot(wsb_ref[hh], vb[:, hh * D_HEAD:(hh + 1) * D_HEAD]) + bias)
        mix = jnp.concatenate(mixes, axis=1)

        x1 = _mixer_merge(x, ga, gb, y, mix, u, wpo_ref, wso_ref, wo_ref)

        tok = lax.broadcasted_iota(jnp.int32, (nb, dec_seq, FF_CHUNK), 1)

        def shifted(j, a):
            lo, hi = j * FF_CHUNK, (j + 1) * FF_CHUNK
            a3 = a.reshape(nb, dec_seq, FF_CHUNK)
            back = stc_ref[0, :, :, lo:hi]
            back2, back1 = back[:, 0:1, :], back[:, 1:2, :]
            s1 = jnp.where(tok == 0, back1, pltpu.roll(a3, 1, axis=1))
            s2 = jnp.where(tok == 0, back2, jnp.where(tok == 1, back1, pltpu.roll(a3, 2, axis=1)))
            return s1.reshape(rt, FF_CHUNK), s2.reshape(rt, FF_CHUNK)

        y_out, a_parts = conv_ffn(x1, shifted)
        ys_ref[...] = y_out
        for j, a in enumerate(a_parts):
            a3 = a.reshape(nb, dec_seq, FF_CHUNK)
            nconvs_ref[0, :, :, j * FF_CHUNK:(j + 1) * FF_CHUNK] = a3[:, dec_seq - (CONV_K - 1):, :]


def _pool_matrices(dec_seq, n_batch):
    a = np.zeros((N_GROUPS, n_batch * dec_seq, n_batch * dec_seq), np.float32)
    b = np.zeros((N_GROUPS, n_batch * dec_seq, POOL_HIST * n_batch), np.float32)
    for g, w in enumerate(POOL_WINDOWS):
        for bb in range(n_batch):
            for t in range(dec_seq):
                for k in range(w):
                    i = POOL_PAD + t - k
                    if i >= POOL_PAD:
                        a[g, bb * dec_seq + t, bb * dec_seq + i - POOL_PAD] = 1.0 / w
                    else:
                        b[g, bb * dec_seq + t, i * n_batch + bb] = 1.0 / w
    return jnp.asarray(a, BF16), jnp.asarray(b, BF16)


def _first_rows_table():
    t = np.arange(POOL_HIST, dtype=np.float32)[:, None]
    w = np.repeat(np.asarray(POOL_WINDOWS, np.float32), D_GROUP)[None, :]
    return jnp.asarray(1.0 / np.minimum(w, t + 1.0), F32), jnp.asarray(1.0 / w, F32)


def _whole(shape):
    zeros = (0,) * len(shape)
    return pl.BlockSpec(shape, lambda i: zeros, pipeline_mode=pl.Buffered(1))


def kernel(x_prompt, x_sample, state_pool, state_ffn_conv, norm1_g, w_in, pool_w, pool_scale,
           w_pool_out, sgu_norm_g, sgu_w, sgu_b, w_sgu_out, w_o, norm2_g, ffn_w_up, ffn_w_gate,
           ffn_conv_w, ffn_conv_b, ffn_w_down, final_norm_g):
    depth = norm1_g.shape[0]
    assert depth == 1
    batch, seq, _ = x_prompt.shape
    dec_batch, dec_seq, _ = x_sample.shape
    assert seq % PROMPT_TILE == 0 and PROMPT_SUBTILE % CHUNK == 0 and PROMPT_SUBTILE >= 2 * POOL_HIST
    assert SAMPLE_TILE % dec_seq == 0 and (dec_batch * dec_seq) % SAMPLE_TILE == 0
    assert CONV_K - 1 <= dec_seq <= CHUNK and dec_seq % SUBLANES == 0
    assert sgu_w.shape[-1] == CHUNK and state_pool.shape[2] == POOL_PAD

    tiles_per_seq = seq // PROMPT_TILE
    n_p = batch * tiles_per_seq
    rows = dec_batch * dec_seq
    n_s = rows // SAMPLE_TILE
    tile_batch = SAMPLE_TILE // dec_seq

    tbl, invw = _first_rows_table()
    pool_a, pool_b = _pool_matrices(dec_seq, tile_batch)
    onehot = np.zeros((CHUNK, SAMPLE_TILE), np.float32)
    onehot[np.arange(SAMPLE_TILE) % dec_seq, np.arange(SAMPLE_TILE)] = 1.0
    onehot = jnp.asarray(onehot, BF16)
    wscale = jnp.asarray(np.where(np.arange(D_IN) < D_POOL, 1.0, 0.5)[None, :], F32)

    p_tile = lambda i: jnp.minimum(i, n_p - 1)
    s_tile = lambda i: jnp.maximum(i - n_p, 0)
    s_rows = lambda n: pl.BlockSpec((SAMPLE_TILE, n), lambda i: (s_tile(i), 0), pipeline_mode=pl.Buffered(1))
    hbm = pl.BlockSpec(memory_space=pl.ANY)

    s_pool = pl.BlockSpec((POOL_PAD, tile_batch, D_POOL), lambda i: (0, s_tile(i), 0),
                          pipeline_mode=pl.Buffered(1))
    s_conv = pl.BlockSpec((1, tile_batch, CONV_K - 1, D_FF), lambda i: (0, s_tile(i), 0, 0),
                          pipeline_mode=pl.Buffered(1))
    tiled = [x_prompt, x_sample.reshape(rows, D_MODEL), jnp.transpose(state_pool[0], (1, 0, 2)),
             state_ffn_conv]
    tiled_specs = [
        pl.BlockSpec((1, PROMPT_TILE, D_MODEL), lambda i: (p_tile(i) // tiles_per_seq, p_tile(i) % tiles_per_seq, 0)),
        s_rows(D_MODEL), s_pool, s_conv]
    small = [norm1_g, pool_scale, sgu_norm_g, sgu_b, norm2_g, jnp.transpose(ffn_conv_w, (1, 0, 2)), ffn_conv_b,
             final_norm_g.reshape(1, D_MODEL), sgu_w, pool_w, invw, tbl, pool_a, pool_b, onehot, wscale]
    big = [w_in, w_pool_out, w_sgu_out, w_o, ffn_w_up, ffn_w_gate, ffn_w_down]

    out_shape = [
        jax.ShapeDtypeStruct((batch, seq, D_MODEL), F32),
        jax.ShapeDtypeStruct((rows, D_MODEL), F32),
        jax.ShapeDtypeStruct((POOL_PAD, batch, D_POOL), F32),
        jax.ShapeDtypeStruct((1, batch, CONV_K - 1, D_FF), F32),
        jax.ShapeDtypeStruct((POOL_PAD, dec_batch, D_POOL), F32),
        jax.ShapeDtypeStruct((1, dec_batch, CONV_K - 1, D_FF), F32),
        jax.ShapeDtypeStruct((rows, D_SGU), F32)]
    out_specs = [
        pl.BlockSpec((1, PROMPT_TILE, D_MODEL), lambda i: (p_tile(i) // tiles_per_seq, p_tile(i) % tiles_per_seq, 0)),
        s_rows(D_MODEL),
        pl.BlockSpec((POOL_PAD, batch, D_POOL), lambda i: (0, 0, 0)),
        pl.BlockSpec((1, 1, CONV_K - 1, D_FF), lambda i: (0, p_tile(i) // tiles_per_seq, 0, 0)),
        s_pool, s_conv, s_rows(D_SGU)]
    scratch = [
        pltpu.VMEM((D_MODEL, D_IN), BF16), pltpu.VMEM((D_POOL, D_MODEL), BF16), pltpu.VMEM((D_SGU, D_MODEL), BF16),
        pltpu.VMEM((D_MODEL, D_MODEL), BF16), pltpu.VMEM((D_MODEL, D_FF), BF16), pltpu.VMEM((D_MODEL, D_FF), BF16),
        pltpu.VMEM((D_FF, D_MODEL), BF16),
        pltpu.VMEM((N_GROUPS // 2, 2 * D_GROUP, 2 * D_GROUP), BF16),
        pltpu.VMEM((N_HEADS, CHUNK, CHUNK), BF16),
        pltpu.VMEM((CHUNK, D_SGU), F32),
        pltpu.VMEM((POOL_HIST, D_POOL), F32), pltpu.VMEM((SUBLANES, D_FF), F32),
        pltpu.VMEM((CONV_K, 1, D_FF), F32), pltpu.VMEM((1, D_FF), F32),
        pltpu.VMEM((N_HEADS, SAMPLE_TILE, SAMPLE_TILE), BF16)]

    outs = pl.pallas_call(
        functools.partial(_layer_kernel, n_prompt_steps=n_p, tiles_per_seq=tiles_per_seq, dec_seq=dec_seq),
        grid=(n_p + n_s,),
        in_specs=tiled_specs + [_whole(a.shape) for a in small] + [hbm] * len(big),
        out_specs=out_specs,
        out_shape=out_shape,
        scratch_shapes=scratch,
        compiler_params=pltpu.CompilerParams(dimension_semantics=("arbitrary",),
                                             vmem_limit_bytes=VMEM_LIMIT_BYTES),
        name="layer_step",
    )(*tiled, *small, *big)
    y_prompt, y_s, npool_p, nconv_p, npool_s, nconv_s, v_s = outs
    return (y_prompt, y_s.reshape(dec_batch, dec_seq, D_MODEL),
            jnp.transpose(npool_p, (1, 0, 2))[None], jnp.transpose(npool_s, (1, 0, 2))[None],
            nconv_p, nconv_s,
            v_s.reshape(1, dec_batch, dec_seq, D_SGU))
```

```python
import functools

import numpy as np
import jax
import jax.numpy as jnp
from jax import lax
from jax.experimental import pallas as pl
from jax.experimental.pallas import tpu as pltpu

D_MODEL = 1024
POOL_WINDOWS = (2, 4, 8, 16)
N_GROUPS = 4
D_GROUP = 128
D_POOL = N_GROUPS * D_GROUP
POOL_PAD = max(POOL_WINDOWS) - 1
POOL_HIST = POOL_PAD + 1
CHUNK = 128
N_HEADS = 4
D_SGU = 512
D_HEAD = D_SGU // N_HEADS
D_IN = D_POOL + 2 * D_SGU + 2 * D_MODEL
_GATE_A = D_POOL + 2 * D_SGU
_GATE_B = _GATE_A + D_MODEL
D_FF = 2816
CONV_K = 3
EPS = 1e-6

SUBLANES = 8
LANES = 128
FF_CHUNK = 256
N_FF_CHUNKS = D_FF // FF_CHUNK
PROMPT_SUBTILE = 256
PROMPT_TILE = 2 * PROMPT_SUBTILE
SAMPLE_TILE = 256
WIDE_ROWS = 128
NARROW_ROWS = 256
LOAD_SLOTS = 6
CAST_ROWS = 16
VMEM_LIMIT_BYTES = 60 * 1024 * 1024

_GELU_C = 0.7978845608028654
_GELU_C3 = _GELU_C * 0.044715

BF16 = jnp.bfloat16
F32 = jnp.float32


def _dot(a, b):
    return jnp.dot(a, b, preferred_element_type=F32)


def _rms(x, g):
    ms = jnp.mean(x * x, axis=-1, keepdims=True)
    return (x * lax.rsqrt(ms + EPS)) * g


def _gelu_of_twice(hx):
    u = hx * (2.0 * _GELU_C + (8.0 * _GELU_C3) * (hx * hx))
    return hx + hx * jnp.tanh(u)


def _split_bf16(x):
    hi = x.astype(BF16)
    lo = (x - hi.astype(F32)).astype(BF16)
    return hi, lo


def _input_proj(h, w_in_ref, sgug_ref):
    p = _dot(h, w_in_ref[:, 0:D_POOL])
    u = _gelu_of_twice(_dot(h, w_in_ref[:, D_POOL:D_POOL + D_SGU]))
    v = _rms(_gelu_of_twice(_dot(h, w_in_ref[:, D_POOL + D_SGU:D_POOL + 2 * D_SGU])), sgug_ref[...])
    ta = jnp.tanh(_dot(h, w_in_ref[:, _GATE_A:_GATE_B]))
    tb = jnp.tanh(_dot(h, w_in_ref[:, _GATE_B:D_IN]))
    return p, u, v, ta, tb


def _pool_groups(d, pw_ref, pscale_ref):
    y01 = _dot(d[:, 0:2 * D_GROUP], pw_ref[0])
    y23 = _dot(d[:, 2 * D_GROUP:], pw_ref[1])
    return (jnp.concatenate([y01, y23], axis=1) * pscale_ref[...]).astype(BF16)


def _mixer_merge(x, ta, tb, y, mix, u, wpo_ref, wso_ref, wo_ref):
    a_out = _dot(y, wpo_ref[...])
    b_out = _dot((u * mix).astype(BF16), wso_ref[...])
    twice_m = ((ta * a_out + a_out) + (tb * b_out + b_out)).astype(BF16)
    return x + _dot(twice_m, wo_ref[...])


def _load_weight_group(weights, chunk_rows, max_cols):
    chunks = [(hbm, vmem, r0, cols, scale)
              for hbm, vmem, rows, cols, scale in weights for r0 in range(0, rows, chunk_rows)]
    lookahead = LOAD_SLOTS - 1

    def run(stage, sem):
        def copy(g):
            hbm, _, r0, cols, _ = chunks[g]
            slot = g % LOAD_SLOTS
            return pltpu.make_async_copy(hbm.at[0, r0:r0 + chunk_rows, :],
                                         stage.at[slot, :, 0:cols], sem.at[slot])

        for g in range(min(lookahead, len(chunks))):
            copy(g).start()
        for g, (_, vmem, r0, cols, scale) in enumerate(chunks):
            if g + lookahead < len(chunks):
                copy(g + lookahead).start()
            copy(g).wait()
            slot = g % LOAD_SLOTS

            def cast(r, carry, vmem=vmem, r0=r0, cols=cols, slot=slot, scale=scale):
                rr = pl.multiple_of(r * CAST_ROWS, CAST_ROWS)
                w = stage[slot, pl.ds(rr, CAST_ROWS), 0:cols]
                if scale is not None:
                    w = w * (scale if isinstance(scale, float) else scale[...])
                vmem[pl.ds(r0 + rr, CAST_ROWS), :] = w.astype(BF16)
                return carry

            lax.fori_loop(0, chunk_rows // CAST_ROWS, cast, 0)

    pl.run_scoped(run, pltpu.VMEM((LOAD_SLOTS, chunk_rows, max_cols), F32),
                  pltpu.SemaphoreType.DMA((LOAD_SLOTS,)))


def _layer_kernel(
        xp_ref, xs_ref, stp_ref, stc_ref,
        g1_ref, pscale_ref, sgug_ref, sgub_ref, g2_ref, cw_ref, cb_ref, gf_ref, sguw_ref, poolw_ref,
        invw_ref, tbl_ref, pool_a_ref, pool_b_ref, onehot_ref, wscale_ref,
        w_in_hbm, wpo_hbm, wso_hbm, wo_hbm, wup_hbm, wgate_hbm, wdown_hbm,
        yp_ref, ys_ref, npoolp_ref, nconvp_ref, npools_ref, nconvs_ref, vs_ref,
        w_in_ref, wpo_ref, wso_ref, wo_ref, wup_ref, wgate_ref, wdown_ref, pw_ref, wsm_ref, bias_ref,
        carry_p, carry_a, hcw_ref, hcb_ref, wsb_ref,
        *, n_prompt_steps, tiles_per_seq, dec_seq):
    step = pl.program_id(0)

    @pl.when(step == 0)
    def _prepare():
        _load_weight_group(
            [(w_in_hbm, w_in_ref, D_MODEL, D_IN, wscale_ref), (wup_hbm, wup_ref, D_MODEL, D_FF, None),
             (wgate_hbm, wgate_ref, D_MODEL, D_FF, None)], WIDE_ROWS, D_IN)
        _load_weight_group(
            [(wdown_hbm, wdown_ref, D_FF, D_MODEL, None), (wo_hbm, wo_ref, D_MODEL, D_MODEL, 0.5),
             (wpo_hbm, wpo_ref, D_POOL, D_MODEL, None), (wso_hbm, wso_ref, D_SGU, D_MODEL, None)],
            NARROW_ROWS, D_MODEL)
        hcw_ref[...] = 0.5 * cw_ref[...]
        hcb_ref[...] = 0.5 * cb_ref[...]
        pw_ref[...] = jnp.zeros_like(pw_ref)
        for g in range(N_GROUPS):
            lo = (g % 2) * D_GROUP
            pw_ref[g // 2, lo:lo + D_GROUP, lo:lo + D_GROUP] = poolw_ref[0, g].astype(BF16)
        row = lax.broadcasted_iota(jnp.int32, (CHUNK, CHUNK), 0)
        col = lax.broadcasted_iota(jnp.int32, (CHUNK, CHUNK), 1)
        for hh in range(N_HEADS):
            wsm_ref[hh] = jnp.where(row >= col, sguw_ref[0, hh], 0.0).astype(BF16)
            b_row = sgub_ref[0, hh:hh + 1, :]
            bias_ref[:, hh * D_HEAD:(hh + 1) * D_HEAD] = jnp.broadcast_to(b_row, (D_HEAD, CHUNK)).T
        rt = SAMPLE_TILE
        row = lax.broadcasted_iota(jnp.int32, (rt, rt), 0)
        col = lax.broadcasted_iota(jnp.int32, (rt, rt), 1)
        keep = ((row // dec_seq) == (col // dec_seq)) & (row >= col)
        for hh in range(N_HEADS):
            w_rows = jnp.tile(sguw_ref[0, hh, 0:dec_seq, :], (rt // dec_seq, 1)).astype(BF16)
            wsb_ref[hh] = jnp.where(keep, _dot(w_rows, onehot_ref[...]), 0.0).astype(BF16)

    def conv_ffn(x1, shifted):
        h2 = _rms(x1, g2_ref[...]).astype(BF16)
        f_parts, a_parts = [], []
        for j in range(N_FF_CHUNKS):
            lo, hi = j * FF_CHUNK, (j + 1) * FF_CHUNK
            a = _dot(h2, wup_ref[:, lo:hi])
            s1, s2 = shifted(j, a)
            half_c = (s2 * hcw_ref[0, :, lo:hi] + s1 * hcw_ref[1, :, lo:hi] + a * hcw_ref[2, :, lo:hi]
                      + hcb_ref[:, lo:hi])
            a_parts.append(a)
            f_parts.append((_gelu_of_twice(half_c) * _dot(h2, wgate_ref[:, lo:hi])).astype(BF16))
        x2 = x1 + _dot(jnp.concatenate(f_parts, axis=1), wdown_ref[...])
        return _rms(x2, gf_ref[...]), a_parts

    @pl.when(step < n_prompt_steps)
    def _prompt():
        st = PROMPT_SUBTILE
        n_sub = PROMPT_TILE // st
        first_tile = (step % tiles_per_seq) == 0

        @pl.when(first_tile)
        def _():
            carry_p[...] = jnp.zeros_like(carry_p)
            carry_a[...] = jnp.zeros_like(carry_a)

        def mixer(x, p_hist, first):
            h = _rms(x, g1_ref[...]).astype(BF16)
            p, u, v, ta, tb = _input_proj(h, w_in_ref, sgug_ref)

            level = jnp.concatenate([p_hist, p], axis=0)
            sums = []
            for g, w in enumerate(POOL_WINDOWS):
                level = level + pltpu.roll(level, w // 2, axis=0)
                sums.append(level[:, 0:D_GROUP])
                if g + 1 < N_GROUPS:
                    level = level[:, D_GROUP:]
            win = jnp.concatenate(sums, axis=1)[POOL_HIST:]
            scaled = jnp.concatenate([win[:POOL_HIST] * first, win[POOL_HIST:] * invw_ref[...]], axis=0)
            y = _pool_groups((scaled - p).astype(BF16), pw_ref, pscale_ref)

            vb = v.astype(BF16)
            n_chunks = st // CHUNK
            per_head = []
            for hh in range(N_HEADS):
                rhs = jnp.concatenate(
                    [vb[c * CHUNK:(c + 1) * CHUNK, hh * D_HEAD:(hh + 1) * D_HEAD] for c in range(n_chunks)],
                    axis=1)
                res = _dot(wsm_ref[hh], rhs)
                bias = bias_ref[:, hh * D_HEAD:(hh + 1) * D_HEAD]
                per_head.append([res[:, c * D_HEAD:(c + 1) * D_HEAD] + bias for c in range(n_chunks)])
            mix = jnp.concatenate(
                [jnp.concatenate([per_head[hh][c] for hh in range(N_HEADS)], axis=1) for c in range(n_chunks)],
                axis=0)
            return _mixer_merge(x, ta, tb, y, mix, u, wpo_ref, wso_ref, wo_ref), p[st - POOL_HIST:]

        invw_rows = jnp.broadcast_to(invw_ref[...], tbl_ref.shape)
        p_hist = carry_p[...]
        x1s = []
        for s in range(n_sub):
            first = jnp.where(first_tile, tbl_ref[...], invw_rows) if s == 0 else invw_rows
            x1, p_hist = mixer(xp_ref[0, s * st:(s + 1) * st], p_hist, first)
            x1s.append(x1)
        carry_p[...] = p_hist
        npoolp_ref[:, pl.ds(step // tiles_per_seq, 1), :] = p_hist[POOL_HIST - POOL_PAD:][:, None, :]

        a_hist = [carry_a[:, j * FF_CHUNK:(j + 1) * FF_CHUNK] for j in range(N_FF_CHUNKS)]
        for s in range(n_sub):
            def shifted(j, a, a_hist=a_hist):
                ext = jnp.concatenate([a_hist[j], a], axis=0)
                return pltpu.roll(ext, 1, axis=0)[SUBLANES:], pltpu.roll(ext, 2, axis=0)[SUBLANES:]

            y_out, a_parts = conv_ffn(x1s[s], shifted)
            a_hist = [a[st - SUBLANES:] for a in a_parts]
            yp_ref[0, s * st:(s + 1) * st] = y_out
        for j in range(N_FF_CHUNKS):
            carry_a[:, j * FF_CHUNK:(j + 1) * FF_CHUNK] = a_hist[j]
            nconvp_ref[0, 0, :, j * FF_CHUNK:(j + 1) * FF_CHUNK] = a_hist[j][SUBLANES - (CONV_K - 1):]

    @pl.when(step >= n_prompt_steps)
    def _sample():
        rt = SAMPLE_TILE
        nb = rt // dec_seq
        x = xs_ref[...]
        h = _rms(x, g1_ref[...]).astype(BF16)
        p, u, v, ta, tb = _input_proj(h, w_in_ref, sgug_ref)
        vs_ref[...] = v

        stp = stp_ref[...]
        old = jnp.concatenate([stp.reshape(POOL_PAD * nb, D_POOL), jnp.zeros((nb, D_POOL), F32)],
                              axis=0)
        p_hi, p_lo = _split_bf16(p)
        s_hi, s_lo = _split_bf16(old)
        means = []
        for g in range(N_GROUPS):
            sl = slice(g * D_GROUP, (g + 1) * D_GROUP)
            new2 = jnp.concatenate([p_hi[:, sl], p_lo[:, sl]], axis=1)
            old2 = jnp.concatenate([s_hi[:, sl], s_lo[:, sl]], axis=1)
            r = _dot(pool_a_ref[g], new2) + _dot(pool_b_ref[g], old2)
            means.append(r[:, :D_GROUP] + r[:, D_GROUP:])
        y = _pool_groups((jnp.concatenate(means, axis=1) - p).astype(BF16), pw_ref, pscale_ref)
        p3 = p.reshape(nb, dec_seq, D_POOL)
        for i in range(POOL_PAD):
            k = i + dec_seq
            npools_ref[i] = stp[k] if k < POOL_PAD else p3[:, k - POOL_PAD, :]

        vb = v.astype(BF16)
        mixes = []
        for hh in range(N_HEADS):
            bias = jnp.tile(bias_ref[0:dec_seq, hh * D_HEAD:(hh + 1) * D_HEAD], (nb, 1))
            mixes.append(_dot(wsb_ref[hh], vb[:, hh * D_HEAD:(hh + 1) * D_HEAD]) + bias)
        mix = jnp.concatenate(mixes, axis=1)

        x1 = _mixer_merge(x, ta, tb, y, mix, u, wpo_ref, wso_ref, wo_ref)

        tok = lax.broadcasted_iota(jnp.int32, (nb, dec_seq, FF_CHUNK), 1)

        def shifted(j, a):
            lo, hi = j * FF_CHUNK, (j + 1) * FF_CHUNK
            a3 = a.reshape(nb, dec_seq, FF_CHUNK)
            back = stc_ref[0, :, :, lo:hi]
            back2, back1 = back[:, 0:1, :], back[:, 1:2, :]
            s1 = jnp.where(tok == 0, back1, pltpu.roll(a3, 1, axis=1))
            s2 = jnp.where(tok == 0, back2, jnp.where(tok == 1, back1, pltpu.roll(a3, 2, axis=1)))
            return s1.reshape(rt, FF_CHUNK), s2.reshape(rt, FF_CHUNK)

        y_out, a_parts = conv_ffn(x1, shifted)
        ys_ref[...] = y_out
        for j, a in enumerate(a_parts):
            a3 = a.reshape(nb, dec_seq, FF_CHUNK)
            nconvs_ref[0, :, :, j * FF_CHUNK:(j + 1) * FF_CHUNK] = a3[:, dec_seq - (CONV_K - 1):, :]


def _pool_matrices(dec_seq, n_batch):
    a = np.zeros((N_GROUPS, n_batch * dec_seq, n_batch * dec_seq), np.float32)
    b = np.zeros((N_GROUPS, n_batch * dec_seq, POOL_HIST * n_batch), np.float32)
    for g, w in enumerate(POOL_WINDOWS):
        for bb in range(n_batch):
            for t in range(dec_seq):
                for k in range(w):
                    i = POOL_PAD + t - k
                    if i >= POOL_PAD:
                        a[g, bb * dec_seq + t, bb * dec_seq + i - POOL_PAD] = 1.0 / w
                    else:
                        b[g, bb * dec_seq + t, i * n_batch + bb] = 1.0 / w
    return jnp.asarray(a, BF16), jnp.asarray(b, BF16)


def _first_rows_table():
    t = np.arange(POOL_HIST, dtype=np.float32)[:, None]
    w = np.repeat(np.asarray(POOL_WINDOWS, np.float32), D_GROUP)[None, :]
    return jnp.asarray(1.0 / np.minimum(w, t + 1.0), F32), jnp.asarray(1.0 / w, F32)


def _whole(shape):
    zeros = (0,) * len(shape)
    return pl.BlockSpec(shape, lambda i: zeros, pipeline_mode=pl.Buffered(1))


def kernel(x_prompt, x_sample, state_pool, state_ffn_conv, norm1_g, w_in, pool_w, pool_scale,
           w_pool_out, sgu_norm_g, sgu_w, sgu_b, w_sgu_out, w_o, norm2_g, ffn_w_up, ffn_w_gate,
           ffn_conv_w, ffn_conv_b, ffn_w_down, final_norm_g):
    depth = norm1_g.shape[0]
    assert depth == 1
    batch, seq, _ = x_prompt.shape
    dec_batch, dec_seq, _ = x_sample.shape
    assert seq % PROMPT_TILE == 0 and PROMPT_SUBTILE % CHUNK == 0 and PROMPT_SUBTILE >= 2 * POOL_HIST
    assert SAMPLE_TILE % dec_seq == 0 and (dec_batch * dec_seq) % SAMPLE_TILE == 0
    assert CONV_K - 1 <= dec_seq <= CHUNK and dec_seq % SUBLANES == 0
    assert sgu_w.shape[-1] == CHUNK and state_pool.shape[2] == POOL_PAD

    tiles_per_seq = seq // PROMPT_TILE
    n_p = batch * tiles_per_seq
    rows = dec_batch * dec_seq
    n_s = rows // SAMPLE_TILE
    tile_batch = SAMPLE_TILE // dec_seq

    tbl, invw = _first_rows_table()
    pool_a, pool_b = _pool_matrices(dec_seq, tile_batch)
    onehot = np.zeros((CHUNK, SAMPLE_TILE), np.float32)
    onehot[np.arange(SAMPLE_TILE) % dec_seq, np.arange(SAMPLE_TILE)] = 1.0
    onehot = jnp.asarray(onehot, BF16)
    wscale = jnp.asarray(np.where(np.arange(D_IN) < D_POOL, 1.0, 0.5)[None, :], F32)

    p_tile = lambda i: jnp.minimum(i, n_p - 1)
    s_tile = lambda i: jnp.maximum(i - n_p, 0)
    s_rows = lambda n: pl.BlockSpec((SAMPLE_TILE, n), lambda i: (s_tile(i), 0), pipeline_mode=pl.Buffered(1))
    hbm = pl.BlockSpec(memory_space=pl.ANY)

    s_pool = pl.BlockSpec((POOL_PAD, tile_batch, D_POOL), lambda i: (0, s_tile(i), 0),
                          pipeline_mode=pl.Buffered(1))
    s_conv = pl.BlockSpec((1, tile_batch, CONV_K - 1, D_FF), lambda i: (0, s_tile(i), 0, 0),
                          pipeline_mode=pl.Buffered(1))
    tiled = [x_prompt, x_sample.reshape(rows, D_MODEL), jnp.transpose(state_pool[0], (1, 0, 2)),
             state_ffn_conv]
    tiled_specs = [
        pl.BlockSpec((1, PROMPT_TILE, D_MODEL), lambda i: (p_tile(i) // tiles_per_seq, p_tile(i) % tiles_per_seq, 0)),
        s_rows(D_MODEL), s_pool, s_conv]
    small = [norm1_g, pool_scale, sgu_norm_g, sgu_b, norm2_g, jnp.transpose(ffn_conv_w, (1, 0, 2)), ffn_conv_b,
             final_norm_g.reshape(1, D_MODEL), sgu_w, pool_w, invw, tbl, pool_a, pool_b, onehot, wscale]
    big = [w_in, w_pool_out, w_sgu_out, w_o, ffn_w_up, ffn_w_gate, ffn_w_down]

    out_shape = [
        jax.ShapeDtypeStruct((batch, seq, D_MODEL), F32),
        jax.ShapeDtypeStruct((rows, D_MODEL), F32),
        jax.ShapeDtypeStruct((POOL_PAD, batch, D_POOL), F32),
        jax.ShapeDtypeStruct((1, batch, CONV_K - 1, D_FF), F32),
        jax.ShapeDtypeStruct((POOL_PAD, dec_batch, D_POOL), F32),
        jax.ShapeDtypeStruct((1, dec_batch, CONV_K - 1, D_FF), F32),
        jax.ShapeDtypeStruct((rows, D_SGU), F32)]
    out_specs = [
        pl.BlockSpec((1, PROMPT_TILE, D_MODEL), lambda i: (p_tile(i) // tiles_per_seq, p_tile(i) % tiles_per_seq, 0)),
        s_rows(D_MODEL),
        pl.BlockSpec((POOL_PAD, batch, D_POOL), lambda i: (0, 0, 0)),
        pl.BlockSpec((1, 1, CONV_K - 1, D_FF), lambda i: (0, p_tile(i) // tiles_per_seq, 0, 0)),
        s_pool, s_conv, s_rows(D_SGU)]
    scratch = [
        pltpu.VMEM((D_MODEL, D_IN), BF16), pltpu.VMEM((D_POOL, D_MODEL), BF16), pltpu.VMEM((D_SGU, D_MODEL), BF16),
        pltpu.VMEM((D_MODEL, D_MODEL), BF16), pltpu.VMEM((D_MODEL, D_FF), BF16), pltpu.VMEM((D_MODEL, D_FF), BF16),
        pltpu.VMEM((D_FF, D_MODEL), BF16),
        pltpu.VMEM((N_GROUPS // 2, 2 * D_GROUP, 2 * D_GROUP), BF16),
        pltpu.VMEM((N_HEADS, CHUNK, CHUNK), BF16),
        pltpu.VMEM((CHUNK, D_SGU), F32),
        pltpu.VMEM((POOL_HIST, D_POOL), F32), pltpu.VMEM((SUBLANES, D_FF), F32),
        pltpu.VMEM((CONV_K, 1, D_FF), F32), pltpu.VMEM((1, D_FF), F32),
        pltpu.VMEM((N_HEADS, SAMPLE_TILE, SAMPLE_TILE), BF16)]

    outs = pl.pallas_call(
        functools.partial(_layer_kernel, n_prompt_steps=n_p, tiles_per_seq=tiles_per_seq, dec_seq=dec_seq),
        grid=(n_p + n_s,),
        in_specs=tiled_specs + [_whole(a.shape) for a in small] + [hbm] * len(big),
        out_specs=out_specs,
        out_shape=out_shape,
        scratch_shapes=scratch,
        compiler_params=pltpu.CompilerParams(dimension_semantics=("arbitrary",),
                                             vmem_limit_bytes=VMEM_LIMIT_BYTES),
        name="layer_step",
    )(*tiled, *small, *big)
    y_prompt, y_s, npool_p, nconv_p, npool_s, nconv_s, v_s = outs
    return (y_prompt, y_s.reshape(dec_batch, dec_seq, D_MODEL),
            jnp.transpose(npool_p, (1, 0, 2))[None], jnp.transpose(npool_s, (1, 0, 2))[None],
            nconv_p, nconv_s,
            v_s.reshape(1, dec_batch, dec_seq, D_SGU))
```

```python
import functools

import numpy as np
import jax
import jax.numpy as jnp
from jax import lax
from jax.experimental import pallas as pl
from jax.experimental.pallas import tpu as pltpu

D_MODEL = 1024
POOL_WINDOWS = (2, 4, 8, 16)
N_GROUPS = 4
D_GROUP = 128
D_POOL = N_GROUPS * D_GROUP
POOL_PAD = max(POOL_WINDOWS) - 1
POOL_HIST = POOL_PAD + 1
CHUNK = 128
N_HEADS = 4
D_SGU = 512
D_HEAD = D_SGU // N_HEADS
D_IN = D_POOL + 2 * D_SGU + 2 * D_MODEL
_GATE_A = D_POOL + 2 * D_SGU
_GATE_B = _GATE_A + D_MODEL
D_FF = 2816
CONV_K = 3
EPS = 1e-6

SUBLANES = 8
LANES = 128
FF_CHUNK = 256
N_FF_CHUNKS = D_FF // FF_CHUNK
PROMPT_SUBTILE = 256
PROMPT_TILE = 2 * PROMPT_SUBTILE
SAMPLE_SUBTILE = 256
SAMPLE_TILE = 2 * SAMPLE_SUBTILE
WIDE_ROWS = 128
NARROW_ROWS = 256
LOAD_SLOTS = 4
CAST_ROWS = 16
VMEM_LIMIT_BYTES = 60 * 1024 * 1024

_GELU_C = 0.7978845608028654
_GELU_C3 = _GELU_C * 0.044715

BF16 = jnp.bfloat16
F32 = jnp.float32


def _dot(a, b):
    return jnp.dot(a, b, preferred_element_type=F32)


def _rms(x, g):
    ms = jnp.mean(x * x, axis=-1, keepdims=True)
    return (x * lax.rsqrt(ms + EPS)) * g


def _gelu_of_twice(hx):
    u = hx * (2.0 * _GELU_C + (8.0 * _GELU_C3) * (hx * hx))
    return hx + hx * jnp.tanh(u)


def _split_bf16(x):
    hi = x.astype(BF16)
    lo = (x - hi.astype(F32)).astype(BF16)
    return hi, lo


def _input_proj(h, w_in_ref, sgug_ref):
    p = _dot(h, w_in_ref[:, 0:D_POOL])
    u = _gelu_of_twice(_dot(h, w_in_ref[:, D_POOL:D_POOL + D_SGU]))
    v = _rms(_gelu_of_twice(_dot(h, w_in_ref[:, D_POOL + D_SGU:D_POOL + 2 * D_SGU])), sgug_ref[...])
    ta = jnp.tanh(_dot(h, w_in_ref[:, _GATE_A:_GATE_B]))
    tb = jnp.tanh(_dot(h, w_in_ref[:, _GATE_B:D_IN]))
    return p, u, v, ta, tb


def _pool_groups(d, pw_ref, pscale_ref):
    y01 = _dot(d[:, 0:2 * D_GROUP], pw_ref[0])
    y23 = _dot(d[:, 2 * D_GROUP:], pw_ref[1])
    return (jnp.concatenate([y01, y23], axis=1) * pscale_ref[...]).astype(BF16)


def _mixer_merge(x, ta, tb, y, mix, u, wpo_ref, wso_ref, wo_ref):
    a_out = _dot(y, wpo_ref[...])
    b_out = _dot((u * mix).astype(BF16), wso_ref[...])
    twice_m = ((ta * a_out + a_out) + (tb * b_out + b_out)).astype(BF16)
    return x + _dot(twice_m, wo_ref[...])


def _load_weight_group(weights, chunk_rows, max_cols):
    chunks = [(hbm, vmem, r0, cols, scale)
              for hbm, vmem, rows, cols, scale in weights for r0 in range(0, rows, chunk_rows)]
    lookahead = LOAD_SLOTS - 1

    def run(stage, sem):
        def copy(g):
            hbm, _, r0, cols, _ = chunks[g]
            slot = g % LOAD_SLOTS
            return pltpu.make_async_copy(hbm.at[0, r0:r0 + chunk_rows, :],
                                         stage.at[slot, :, 0:cols], sem.at[slot])

        for g in range(min(lookahead, len(chunks))):
            copy(g).start()
        for g, (_, vmem, r0, cols, scale) in enumerate(chunks):
            if g + lookahead < len(chunks):
                copy(g + lookahead).start()
            copy(g).wait()
            slot = g % LOAD_SLOTS

            def cast(r, carry, vmem=vmem, r0=r0, cols=cols, slot=slot, scale=scale):
                rr = pl.multiple_of(r * CAST_ROWS, CAST_ROWS)
                w = stage[slot, pl.ds(rr, CAST_ROWS), 0:cols]
                if scale is not None:
                    w = w * (scale if isinstance(scale, float) else scale[...])
                vmem[pl.ds(r0 + rr, CAST_ROWS), :] = w.astype(BF16)
                return carry

            lax.fori_loop(0, chunk_rows // CAST_ROWS, cast, 0)

    pl.run_scoped(run, pltpu.VMEM((LOAD_SLOTS, chunk_rows, max_cols), F32),
                  pltpu.SemaphoreType.DMA((LOAD_SLOTS,)))


def _layer_kernel(
        xp_ref, xs_ref, stp_ref, stc_ref,
        g1_ref, pscale_ref, sgug_ref, sgub_ref, g2_ref, cw_ref, cb_ref, gf_ref, sguw_ref, poolw_ref,
        invw_ref, tbl_ref, pool_a_ref, pool_b_ref, onehot_ref, wscale_ref,
        w_in_hbm, wpo_hbm, wso_hbm, wo_hbm, wup_hbm, wgate_hbm, wdown_hbm,
        yp_ref, ys_ref, npoolp_ref, nconvp_ref, npools_ref, nconvs_ref, vs_ref,
        w_in_ref, wpo_ref, wso_ref, wo_ref, wup_ref, wgate_ref, wdown_ref, pw_ref, wsm_ref, bias_ref,
        carry_p, carry_a, hcw_ref, hcb_ref, wsb_ref,
        *, n_prompt_steps, tiles_per_seq, dec_seq):
    step = pl.program_id(0)

    @pl.when(step == 0)
    def _prepare():
        _load_weight_group(
            [(w_in_hbm, w_in_ref, D_MODEL, D_IN, wscale_ref), (wup_hbm, wup_ref, D_MODEL, D_FF, None),
             (wgate_hbm, wgate_ref, D_MODEL, D_FF, None)], WIDE_ROWS, D_IN)
        _load_weight_group(
            [(wdown_hbm, wdown_ref, D_FF, D_MODEL, None), (wo_hbm, wo_ref, D_MODEL, D_MODEL, 0.5),
             (wpo_hbm, wpo_ref, D_POOL, D_MODEL, None), (wso_hbm, wso_ref, D_SGU, D_MODEL, None)],
            NARROW_ROWS, D_MODEL)
        hcw_ref[...] = 0.5 * cw_ref[...]
        hcb_ref[...] = 0.5 * cb_ref[...]
        pw_ref[...] = jnp.zeros_like(pw_ref)
        for g in range(N_GROUPS):
            lo = (g % 2) * D_GROUP
            pw_ref[g // 2, lo:lo + D_GROUP, lo:lo + D_GROUP] = poolw_ref[0, g].astype(BF16)
        row = lax.broadcasted_iota(jnp.int32, (CHUNK, CHUNK), 0)
        col = lax.broadcasted_iota(jnp.int32, (CHUNK, CHUNK), 1)
        for hh in range(N_HEADS):
            wsm_ref[hh] = jnp.where(row >= col, sguw_ref[0, hh], 0.0).astype(BF16)
            b_row = sgub_ref[0, hh:hh + 1, :]
            bias_ref[:, hh * D_HEAD:(hh + 1) * D_HEAD] = jnp.broadcast_to(b_row, (D_HEAD, CHUNK)).T
        rt = SAMPLE_SUBTILE
        row = lax.broadcasted_iota(jnp.int32, (rt, rt), 0)
        col = lax.broadcasted_iota(jnp.int32, (rt, rt), 1)
        keep = ((row // dec_seq) == (col // dec_seq)) & (row >= col)
        for hh in range(N_HEADS):
            w_rows = jnp.tile(sguw_ref[0, hh, 0:dec_seq, :], (rt // dec_seq, 1)).astype(BF16)
            wsb_ref[hh] = jnp.where(keep, _dot(w_rows, onehot_ref[...]), 0.0).astype(BF16)

    def conv_ffn(x1, shifted):
        h2 = _rms(x1, g2_ref[...]).astype(BF16)
        f_parts, a_parts = [], []
        for j in range(N_FF_CHUNKS):
            lo, hi = j * FF_CHUNK, (j + 1) * FF_CHUNK
            a = _dot(h2, wup_ref[:, lo:hi])
            s1, s2 = shifted(j, a)
            half_c = (s2 * hcw_ref[0, :, lo:hi] + s1 * hcw_ref[1, :, lo:hi] + a * hcw_ref[2, :, lo:hi]
                      + hcb_ref[:, lo:hi])
            a_parts.append(a)
            f_parts.append((_gelu_of_twice(half_c) * _dot(h2, wgate_ref[:, lo:hi])).astype(BF16))
        x2 = x1 + _dot(jnp.concatenate(f_parts, axis=1), wdown_ref[...])
        return _rms(x2, gf_ref[...]), a_parts

    @pl.when(step < n_prompt_steps)
    def _prompt():
        st = PROMPT_SUBTILE
        n_sub = PROMPT_TILE // st
        first_tile = (step % tiles_per_seq) == 0

        @pl.when(first_tile)
        def _():
            carry_p[...] = jnp.zeros_like(carry_p)
            carry_a[...] = jnp.zeros_like(carry_a)

        def mixer(x, p_hist, first):
            h = _rms(x, g1_ref[...]).astype(BF16)
            p, u, v, ta, tb = _input_proj(h, w_in_ref, sgug_ref)

            level = jnp.concatenate([p_hist, p], axis=0)
            sums = []
            for g, w in enumerate(POOL_WINDOWS):
                level = level + pltpu.roll(level, w // 2, axis=0)
                sums.append(level[:, 0:D_GROUP])
                if g + 1 < N_GROUPS:
                    level = level[:, D_GROUP:]
            win = jnp.concatenate(sums, axis=1)[POOL_HIST:]
            scaled = jnp.concatenate([win[:POOL_HIST] * first, win[POOL_HIST:] * invw_ref[...]], axis=0)
            y = _pool_groups((scaled - p).astype(BF16), pw_ref, pscale_ref)

            vb = v.astype(BF16)
            n_chunks = st // CHUNK
            per_head = []
            for hh in range(N_HEADS):
                rhs = jnp.concatenate(
                    [vb[c * CHUNK:(c + 1) * CHUNK, hh * D_HEAD:(hh + 1) * D_HEAD] for c in range(n_chunks)],
                    axis=1)
                res = _dot(wsm_ref[hh], rhs)
                bias = bias_ref[:, hh * D_HEAD:(hh + 1) * D_HEAD]
                per_head.append([res[:, c * D_HEAD:(c + 1) * D_HEAD] + bias for c in range(n_chunks)])
            mix = jnp.concatenate(
                [jnp.concatenate([per_head[hh][c] for hh in range(N_HEADS)], axis=1) for c in range(n_chunks)],
                axis=0)
            return _mixer_merge(x, ta, tb, y, mix, u, wpo_ref, wso_ref, wo_ref), p[st - POOL_HIST:]

        invw_rows = jnp.broadcast_to(invw_ref[...], tbl_ref.shape)
        p_hist = carry_p[...]
        x1s = []
        for s in range(n_sub):
            first = jnp.where(first_tile, tbl_ref[...], invw_rows) if s == 0 else invw_rows
            x1, p_hist = mixer(xp_ref[0, s * st:(s + 1) * st], p_hist, first)
            x1s.append(x1)
        carry_p[...] = p_hist
        npoolp_ref[:, pl.ds(step // tiles_per_seq, 1), :] = p_hist[POOL_HIST - POOL_PAD:][:, None, :]

        a_hist = [carry_a[:, j * FF_CHUNK:(j + 1) * FF_CHUNK] for j in range(N_FF_CHUNKS)]
        for s in range(n_sub):
            def shifted(j, a, a_hist=a_hist):
                ext = jnp.concatenate([a_hist[j], a], axis=0)
                return pltpu.roll(ext, 1, axis=0)[SUBLANES:], pltpu.roll(ext, 2, axis=0)[SUBLANES:]

            y_out, a_parts = conv_ffn(x1s[s], shifted)
            a_hist = [a[st - SUBLANES:] for a in a_parts]
            yp_ref[0, s * st:(s + 1) * st] = y_out
        for j in range(N_FF_CHUNKS):
            carry_a[:, j * FF_CHUNK:(j + 1) * FF_CHUNK] = a_hist[j]
            nconvp_ref[0, 0, :, j * FF_CHUNK:(j + 1) * FF_CHUNK] = a_hist[j][SUBLANES - (CONV_K - 1):]

    @pl.when(step >= n_prompt_steps)
    def _sample():
        rt = SAMPLE_SUBTILE
        nb = rt // dec_seq

        def mixer(s):
            rows, bat = slice(s * rt, (s + 1) * rt), slice(s * nb, (s + 1) * nb)
            x = xs_ref[rows]
            h = _rms(x, g1_ref[...]).astype(BF16)
            p, u, v, ta, tb = _input_proj(h, w_in_ref, sgug_ref)
            vs_ref[rows] = v

            stp = stp_ref[:, bat, :]
            old = jnp.concatenate([stp.reshape(POOL_PAD * nb, D_POOL), jnp.zeros((nb, D_POOL), F32)],
                                  axis=0)
            p_hi, p_lo = _split_bf16(p)
            s_hi, s_lo = _split_bf16(old)
            means = []
            for g in range(N_GROUPS):
                sl = slice(g * D_GROUP, (g + 1) * D_GROUP)
                new2 = jnp.concatenate([p_hi[:, sl], p_lo[:, sl]], axis=1)
                old2 = jnp.concatenate([s_hi[:, sl], s_lo[:, sl]], axis=1)
                r = _dot(pool_a_ref[g], new2) + _dot(pool_b_ref[g], old2)
                means.append(r[:, :D_GROUP] + r[:, D_GROUP:])
            y = _pool_groups((jnp.concatenate(means, axis=1) - p).astype(BF16), pw_ref, pscale_ref)
            p3 = p.reshape(nb, dec_seq, D_POOL)
            for i in range(POOL_PAD):
                k = i + dec_seq
                npools_ref[i, bat, :] = stp[k] if k < POOL_PAD else p3[:, k - POOL_PAD, :]

            vb = v.astype(BF16)
            mixes = []
            for hh in range(N_HEADS):
                bias = jnp.tile(bias_ref[0:dec_seq, hh * D_HEAD:(hh + 1) * D_HEAD], (nb, 1))
                mixes.append(_dot(wsb_ref[hh], vb[:, hh * D_HEAD:(hh + 1) * D_HEAD]) + bias)
            mix = jnp.concatenate(mixes, axis=1)
            return _mixer_merge(x, ta, tb, y, mix, u, wpo_ref, wso_ref, wo_ref)

        tok = lax.broadcasted_iota(jnp.int32, (nb, dec_seq, FF_CHUNK), 1)

        def ffn(s, x1):
            rows, bat = slice(s * rt, (s + 1) * rt), slice(s * nb, (s + 1) * nb)

            def shifted(j, a):
                lo, hi = j * FF_CHUNK, (j + 1) * FF_CHUNK
                a3 = a.reshape(nb, dec_seq, FF_CHUNK)
                back = stc_ref[0, bat, :, lo:hi]
                back2, back1 = back[:, 0:1, :], back[:, 1:2, :]
                s1 = jnp.where(tok == 0, back1, pltpu.roll(a3, 1, axis=1))
                s2 = jnp.where(tok == 0, back2, jnp.where(tok == 1, back1, pltpu.roll(a3, 2, axis=1)))
                return s1.reshape(rt, FF_CHUNK), s2.reshape(rt, FF_CHUNK)

            y_out, a_parts = conv_ffn(x1, shifted)
            ys_ref[rows] = y_out
            for j, a in enumerate(a_parts):
                a3 = a.reshape(nb, dec_seq, FF_CHUNK)
                nconvs_ref[0, bat, :, j * FF_CHUNK:(j + 1) * FF_CHUNK] = a3[:, dec_seq - (CONV_K - 1):, :]

        n_sub = SAMPLE_TILE // rt
        x1s = [mixer(s) for s in range(n_sub)]
        for s in range(n_sub):
            ffn(s, x1s[s])


def _pool_matrices(dec_seq, n_batch):
    a = np.zeros((N_GROUPS, n_batch * dec_seq, n_batch * dec_seq), np.float32)
    b = np.zeros((N_GROUPS, n_batch * dec_seq, POOL_HIST * n_batch), np.float32)
    for g, w in enumerate(POOL_WINDOWS):
        for bb in range(n_batch):
            for t in range(dec_seq):
                for k in range(w):
                    i = POOL_PAD + t - k
                    if i >= POOL_PAD:
                        a[g, bb * dec_seq + t, bb * dec_seq + i - POOL_PAD] = 1.0 / w
                    else:
                        b[g, bb * dec_seq + t, i * n_batch + bb] = 1.0 / w
    return jnp.asarray(a, BF16), jnp.asarray(b, BF16)


def _first_rows_table():
    t = np.arange(POOL_HIST, dtype=np.float32)[:, None]
    w = np.repeat(np.asarray(POOL_WINDOWS, np.float32), D_GROUP)[None, :]
    return jnp.asarray(1.0 / np.minimum(w, t + 1.0), F32), jnp.asarray(1.0 / w, F32)


def _whole(shape):
    zeros = (0,) * len(shape)
    return pl.BlockSpec(shape, lambda i: zeros, pipeline_mode=pl.Buffered(1))


def kernel(x_prompt, x_sample, state_pool, state_ffn_conv, norm1_g, w_in, pool_w, pool_scale,
           w_pool_out, sgu_norm_g, sgu_w, sgu_b, w_sgu_out, w_o, norm2_g, ffn_w_up, ffn_w_gate,
           ffn_conv_w, ffn_conv_b, ffn_w_down, final_norm_g):
    depth = norm1_g.shape[0]
    assert depth == 1
    batch, seq, _ = x_prompt.shape
    dec_batch, dec_seq, _ = x_sample.shape
    assert seq % PROMPT_TILE == 0 and PROMPT_SUBTILE % CHUNK == 0 and PROMPT_SUBTILE >= 2 * POOL_HIST
    assert SAMPLE_SUBTILE % dec_seq == 0 and (dec_batch * dec_seq) % SAMPLE_TILE == 0
    assert CONV_K - 1 <= dec_seq <= CHUNK and dec_seq % SUBLANES == 0
    assert sgu_w.shape[-1] == CHUNK and state_pool.shape[2] == POOL_PAD

    tiles_per_seq = seq // PROMPT_TILE
    n_p = batch * tiles_per_seq
    rows = dec_batch * dec_seq
    n_s = rows // SAMPLE_TILE
    tile_batch = SAMPLE_TILE // dec_seq

    tbl, invw = _first_rows_table()
    pool_a, pool_b = _pool_matrices(dec_seq, SAMPLE_SUBTILE // dec_seq)
    onehot = np.zeros((CHUNK, SAMPLE_SUBTILE), np.float32)
    onehot[np.arange(SAMPLE_SUBTILE) % dec_seq, np.arange(SAMPLE_SUBTILE)] = 1.0
    onehot = jnp.asarray(onehot, BF16)
    wscale = jnp.asarray(np.where(np.arange(D_IN) < D_POOL, 1.0, 0.5)[None, :], F32)

    p_tile = lambda i: jnp.minimum(i, n_p - 1)
    s_tile = lambda i: jnp.maximum(i - n_p, 0)
    s_rows = lambda n: pl.BlockSpec((SAMPLE_TILE, n), lambda i: (s_tile(i), 0), pipeline_mode=pl.Buffered(1))
    hbm = pl.BlockSpec(memory_space=pl.ANY)

    s_pool = pl.BlockSpec((POOL_PAD, tile_batch, D_POOL), lambda i: (0, s_tile(i), 0),
                          pipeline_mode=pl.Buffered(1))
    s_conv = pl.BlockSpec((1, tile_batch, CONV_K - 1, D_FF), lambda i: (0, s_tile(i), 0, 0),
                          pipeline_mode=pl.Buffered(1))
    tiled = [x_prompt, x_sample.reshape(rows, D_MODEL), jnp.transpose(state_pool[0], (1, 0, 2)),
             state_ffn_conv]
    tiled_specs = [
        pl.BlockSpec((1, PROMPT_TILE, D_MODEL), lambda i: (p_tile(i) // tiles_per_seq, p_tile(i) % tiles_per_seq, 0)),
        s_rows(D_MODEL), s_pool, s_conv]
    small = [norm1_g, pool_scale, sgu_norm_g, sgu_b, norm2_g, jnp.transpose(ffn_conv_w, (1, 0, 2)), ffn_conv_b,
             final_norm_g.reshape(1, D_MODEL), sgu_w, pool_w, invw, tbl, pool_a, pool_b, onehot, wscale]
    big = [w_in, w_pool_out, w_sgu_out, w_o, ffn_w_up, ffn_w_gate, ffn_w_down]

    out_shape = [
        jax.ShapeDtypeStruct((batch, seq, D_MODEL), F32),
        jax.ShapeDtypeStruct((rows, D_MODEL), F32),
        jax.ShapeDtypeStruct((POOL_PAD, batch, D_POOL), F32),
        jax.ShapeDtypeStruct((1, batch, CONV_K - 1, D_FF), F32),
        jax.ShapeDtypeStruct((POOL_PAD, dec_batch, D_POOL), F32),
        jax.ShapeDtypeStruct((1, dec_batch, CONV_K - 1, D_FF), F32),
        jax.ShapeDtypeStruct((rows, D_SGU), F32)]
    out_specs = [
        pl.BlockSpec((1, PROMPT_TILE, D_MODEL), lambda i: (p_tile(i) // tiles_per_seq, p_tile(i) % tiles_per_seq, 0)),
        s_rows(D_MODEL),
        pl.BlockSpec((POOL_PAD, batch, D_POOL), lambda i: (0, 0, 0)),
        pl.BlockSpec((1, 1, CONV_K - 1, D_FF), lambda i: (0, p_tile(i) // tiles_per_seq, 0, 0)),
        s_pool, s_conv, s_rows(D_SGU)]
    scratch = [
        pltpu.VMEM((D_MODEL, D_IN), BF16), pltpu.VMEM((D_POOL, D_MODEL), BF16), pltpu.VMEM((D_SGU, D_MODEL), BF16),
        pltpu.VMEM((D_MODEL, D_MODEL), BF16), pltpu.VMEM((D_MODEL, D_FF), BF16), pltpu.VMEM((D_MODEL, D_FF), BF16),
        pltpu.VMEM((D_FF, D_MODEL), BF16),
        pltpu.VMEM((N_GROUPS // 2, 2 * D_GROUP, 2 * D_GROUP), BF16),
        pltpu.VMEM((N_HEADS, CHUNK, CHUNK), BF16),
        pltpu.VMEM((CHUNK, D_SGU), F32),
        pltpu.VMEM((POOL_HIST, D_POOL), F32), pltpu.VMEM((SUBLANES, D_FF), F32),
        pltpu.VMEM((CONV_K, 1, D_FF), F32), pltpu.VMEM((1, D_FF), F32),
        pltpu.VMEM((N_HEADS, SAMPLE_SUBTILE, SAMPLE_SUBTILE), BF16)]

    outs = pl.pallas_call(
        functools.partial(_layer_kernel, n_prompt_steps=n_p, tiles_per_seq=tiles_per_seq, dec_seq=dec_seq),
        grid=(n_p + n_s,),
        in_specs=tiled_specs + [_whole(a.shape) for a in small] + [hbm] * len(big),
        out_specs=out_specs,
        out_shape=out_shape,
        scratch_shapes=scratch,
        compiler_params=pltpu.CompilerParams(dimension_semantics=("arbitrary",),
                                             vmem_limit_bytes=VMEM_LIMIT_BYTES),
        name="layer_step",
    )(*tiled, *small, *big)
    y_prompt, y_s, npool_p, nconv_p, npool_s, nconv_s, v_s = outs
    return (y_prompt, y_s.reshape(dec_batch, dec_seq, D_MODEL),
            jnp.transpose(npool_p, (1, 0, 2))[None], jnp.transpose(npool_s, (1, 0, 2))[None],
            nconv_p, nconv_s,
            v_s.reshape(1, dec_batch, dec_seq, D_SGU))
```

```python
import functools

import numpy as np
import jax
import jax.numpy as jnp
from jax import lax
from jax.experimental import pallas as pl
from jax.experimental.pallas import tpu as pltpu

D_MODEL = 1024
POOL_WINDOWS = (2, 4, 8, 16)
N_GROUPS = 4
D_GROUP = 128
D_POOL = N_GROUPS * D_GROUP
POOL_PAD = max(POOL_WINDOWS) - 1
POOL_HIST = POOL_PAD + 1
CHUNK = 128
N_HEADS = 4
D_SGU = 512
D_HEAD = D_SGU // N_HEADS
D_IN = D_POOL + 2 * D_SGU + 2 * D_MODEL
_GATE_A = D_POOL + 2 * D_SGU
_GATE_B = _GATE_A + D_MODEL
D_FF = 2816
CONV_K = 3
EPS = 1e-6

SUBLANES = 8
LANES = 128
FF_CHUNK = 256
N_FF_CHUNKS = D_FF // FF_CHUNK
PROMPT_SUBTILE = 512
PROMPT_TILE = 1 * PROMPT_SUBTILE
SAMPLE_SUBTILE = 256
SAMPLE_TILE = 2 * SAMPLE_SUBTILE
WIDE_ROWS = 128
NARROW_ROWS = 256
LOAD_SLOTS = 4
CAST_ROWS = 16
VMEM_LIMIT_BYTES = 60 * 1024 * 1024

_GELU_C = 0.7978845608028654
_GELU_C3 = _GELU_C * 0.044715

BF16 = jnp.bfloat16
F32 = jnp.float32


def _dot(a, b):
    return jnp.dot(a, b, preferred_element_type=F32)


def _rms(x, g):
    ms = jnp.mean(x * x, axis=-1, keepdims=True)
    return (x * lax.rsqrt(ms + EPS)) * g


def _gelu_of_twice(hx):
    u = hx * (2.0 * _GELU_C + (8.0 * _GELU_C3) * (hx * hx))
    return hx + hx * jnp.tanh(u)


def _split_bf16(x):
    hi = x.astype(BF16)
    lo = (x - hi.astype(F32)).astype(BF16)
    return hi, lo


def _input_proj(h, w_in_ref, sgug_ref):
    p = _dot(h, w_in_ref[:, 0:D_POOL])
    u = _gelu_of_twice(_dot(h, w_in_ref[:, D_POOL:D_POOL + D_SGU]))
    v = _rms(_gelu_of_twice(_dot(h, w_in_ref[:, D_POOL + D_SGU:D_POOL + 2 * D_SGU])), sgug_ref[...])
    ta = jnp.tanh(_dot(h, w_in_ref[:, _GATE_A:_GATE_B]))
    tb = jnp.tanh(_dot(h, w_in_ref[:, _GATE_B:D_IN]))
    return p, u, v, ta, tb


def _pool_groups(d, pw_ref, pscale_ref):
    y01 = _dot(d[:, 0:2 * D_GROUP], pw_ref[0])
    y23 = _dot(d[:, 2 * D_GROUP:], pw_ref[1])
    return (jnp.concatenate([y01, y23], axis=1) * pscale_ref[...]).astype(BF16)


def _mixer_merge(x, ta, tb, y, mix, u, wpo_ref, wso_ref, wo_ref):
    a_out = _dot(y, wpo_ref[...])
    b_out = _dot((u * mix).astype(BF16), wso_ref[...])
    twice_m = ((ta * a_out + a_out) + (tb * b_out + b_out)).astype(BF16)
    return x + _dot(twice_m, wo_ref[...])


def _load_weight_group(weights, chunk_rows, max_cols):
    chunks = [(hbm, vmem, r0, cols, scale)
              for hbm, vmem, rows, cols, scale in weights for r0 in range(0, rows, chunk_rows)]
    lookahead = LOAD_SLOTS - 1

    def run(stage, sem):
        def copy(g):
            hbm, _, r0, cols, _ = chunks[g]
            slot = g % LOAD_SLOTS
            return pltpu.make_async_copy(hbm.at[0, r0:r0 + chunk_rows, :],
                                         stage.at[slot, :, 0:cols], sem.at[slot])

        for g in range(min(lookahead, len(chunks))):
            copy(g).start()
        for g, (_, vmem, r0, cols, scale) in enumerate(chunks):
            if g + lookahead < len(chunks):
                copy(g + lookahead).start()
            copy(g).wait()
            slot = g % LOAD_SLOTS

            def cast(r, carry, vmem=vmem, r0=r0, cols=cols, slot=slot, scale=scale):
                rr = pl.multiple_of(r * CAST_ROWS, CAST_ROWS)
                w = stage[slot, pl.ds(rr, CAST_ROWS), 0:cols]
                if scale is not None:
                    w = w * (scale if isinstance(scale, float) else scale[...])
                vmem[pl.ds(r0 + rr, CAST_ROWS), :] = w.astype(BF16)
                return carry

            lax.fori_loop(0, chunk_rows // CAST_ROWS, cast, 0)

    pl.run_scoped(run, pltpu.VMEM((LOAD_SLOTS, chunk_rows, max_cols), F32),
                  pltpu.SemaphoreType.DMA((LOAD_SLOTS,)))


def _layer_kernel(
        xp_ref, xs_ref, stp_ref, stc_ref,
        g1_ref, pscale_ref, sgug_ref, sgub_ref, g2_ref, cw_ref, cb_ref, gf_ref, sguw_ref, poolw_ref,
        invw_ref, tbl_ref, pool_a_ref, pool_b_ref, onehot_ref, wscale_ref,
        w_in_hbm, wpo_hbm, wso_hbm, wo_hbm, wup_hbm, wgate_hbm, wdown_hbm,
        yp_ref, ys_ref, npoolp_ref, nconvp_ref, npools_ref, nconvs_ref, vs_ref,
        w_in_ref, wpo_ref, wso_ref, wo_ref, wup_ref, wgate_ref, wdown_ref, pw_ref, wsm_ref, bias_ref,
        carry_p, carry_a, hcw_ref, hcb_ref, wsb_ref,
        *, n_prompt_steps, tiles_per_seq, dec_seq):
    step = pl.program_id(0)

    @pl.when(step == 0)
    def _prepare():
        _load_weight_group(
            [(w_in_hbm, w_in_ref, D_MODEL, D_IN, wscale_ref), (wup_hbm, wup_ref, D_MODEL, D_FF, None),
             (wgate_hbm, wgate_ref, D_MODEL, D_FF, None)], WIDE_ROWS, D_IN)
        _load_weight_group(
            [(wdown_hbm, wdown_ref, D_FF, D_MODEL, None), (wo_hbm, wo_ref, D_MODEL, D_MODEL, 0.5),
             (wpo_hbm, wpo_ref, D_POOL, D_MODEL, None), (wso_hbm, wso_ref, D_SGU, D_MODEL, None)],
            NARROW_ROWS, D_MODEL)
        hcw_ref[...] = 0.5 * cw_ref[...]
        hcb_ref[...] = 0.5 * cb_ref[...]
        pw_ref[...] = jnp.zeros_like(pw_ref)
        for g in range(N_GROUPS):
            lo = (g % 2) * D_GROUP
            pw_ref[g // 2, lo:lo + D_GROUP, lo:lo + D_GROUP] = poolw_ref[0, g].astype(BF16)
        row = lax.broadcasted_iota(jnp.int32, (CHUNK, CHUNK), 0)
        col = lax.broadcasted_iota(jnp.int32, (CHUNK, CHUNK), 1)
        for hh in range(N_HEADS):
            wsm_ref[hh] = jnp.where(row >= col, sguw_ref[0, hh], 0.0).astype(BF16)
            b_row = sgub_ref[0, hh:hh + 1, :]
            bias_ref[:, hh * D_HEAD:(hh + 1) * D_HEAD] = jnp.broadcast_to(b_row, (D_HEAD, CHUNK)).T
        rt = SAMPLE_SUBTILE
        row = lax.broadcasted_iota(jnp.int32, (rt, rt), 0)
        col = lax.broadcasted_iota(jnp.int32, (rt, rt), 1)
        keep = ((row // dec_seq) == (col // dec_seq)) & (row >= col)
        for hh in range(N_HEADS):
            w_rows = jnp.tile(sguw_ref[0, hh, 0:dec_seq, :], (rt // dec_seq, 1)).astype(BF16)
            wsb_ref[hh] = jnp.where(keep, _dot(w_rows, onehot_ref[...]), 0.0).astype(BF16)

    def conv_ffn(x1, shifted):
        h2 = _rms(x1, g2_ref[...]).astype(BF16)
        f_parts, a_parts = [], []
        for j in range(N_FF_CHUNKS):
            lo, hi = j * FF_CHUNK, (j + 1) * FF_CHUNK
            a = _dot(h2, wup_ref[:, lo:hi])
            s1, s2 = shifted(j, a)
            half_c = (s2 * hcw_ref[0, :, lo:hi] + s1 * hcw_ref[1, :, lo:hi] + a * hcw_ref[2, :, lo:hi]
                      + hcb_ref[:, lo:hi])
            a_parts.append(a)
            f_parts.append((_gelu_of_twice(half_c) * _dot(h2, wgate_ref[:, lo:hi])).astype(BF16))
        x2 = x1 + _dot(jnp.concatenate(f_parts, axis=1), wdown_ref[...])
        return _rms(x2, gf_ref[...]), a_parts

    @pl.when(step < n_prompt_steps)
    def _prompt():
        st = PROMPT_SUBTILE
        n_sub = PROMPT_TILE // st
        first_tile = (step % tiles_per_seq) == 0

        @pl.when(first_tile)
        def _():
            carry_p[...] = jnp.zeros_like(carry_p)
            carry_a[...] = jnp.zeros_like(carry_a)

        def mixer(x, p_hist, first):
            h = _rms(x, g1_ref[...]).astype(BF16)
            p, u, v, ta, tb = _input_proj(h, w_in_ref, sgug_ref)

            level = jnp.concatenate([p_hist, p], axis=0)
            sums = []
            for g, w in enumerate(POOL_WINDOWS):
                level = level + pltpu.roll(level, w // 2, axis=0)
                sums.append(level[:, 0:D_GROUP])
                if g + 1 < N_GROUPS:
                    level = level[:, D_GROUP:]
            win = jnp.concatenate(sums, axis=1)[POOL_HIST:]
            scaled = jnp.concatenate([win[:POOL_HIST] * first, win[POOL_HIST:] * invw_ref[...]], axis=0)
            y = _pool_groups((scaled - p).astype(BF16), pw_ref, pscale_ref)

            vb = v.astype(BF16)
            n_chunks = st // CHUNK
            per_head = []
            for hh in range(N_HEADS):
                rhs = jnp.concatenate(
                    [vb[c * CHUNK:(c + 1) * CHUNK, hh * D_HEAD:(hh + 1) * D_HEAD] for c in range(n_chunks)],
                    axis=1)
                res = _dot(wsm_ref[hh], rhs)
                bias = bias_ref[:, hh * D_HEAD:(hh + 1) * D_HEAD]
                per_head.append([res[:, c * D_HEAD:(c + 1) * D_HEAD] + bias for c in range(n_chunks)])
            mix = jnp.concatenate(
                [jnp.concatenate([per_head[hh][c] for hh in range(N_HEADS)], axis=1) for c in range(n_chunks)],
                axis=0)
            return _mixer_merge(x, ta, tb, y, mix, u, wpo_ref, wso_ref, wo_ref), p[st - POOL_HIST:]

        invw_rows = jnp.broadcast_to(invw_ref[...], tbl_ref.shape)
        p_hist = carry_p[...]
        x1s = []
        for s in range(n_sub):
            first = jnp.where(first_tile, tbl_ref[...], invw_rows) if s == 0 else invw_rows
            x1, p_hist = mixer(xp_ref[0, s * st:(s + 1) * st], p_hist, first)
            x1s.append(x1)
        carry_p[...] = p_hist
        npoolp_ref[:, pl.ds(step // tiles_per_seq, 1), :] = p_hist[POOL_HIST - POOL_PAD:][:, None, :]

        a_hist = [carry_a[:, j * FF_CHUNK:(j + 1) * FF_CHUNK] for j in range(N_FF_CHUNKS)]
        for s in range(n_sub):
            def shifted(j, a, a_hist=a_hist):
                ext = jnp.concatenate([a_hist[j], a], axis=0)
                return pltpu.roll(ext, 1, axis=0)[SUBLANES:], pltpu.roll(ext, 2, axis=0)[SUBLANES:]

            y_out, a_parts = conv_ffn(x1s[s], shifted)
            a_hist = [a[st - SUBLANES:] for a in a_parts]
            yp_ref[0, s * st:(s + 1) * st] = y_out
        for j in range(N_FF_CHUNKS):
            carry_a[:, j * FF_CHUNK:(j + 1) * FF_CHUNK] = a_hist[j]
            nconvp_ref[0, 0, :, j * FF_CHUNK:(j + 1) * FF_CHUNK] = a_hist[j][SUBLANES - (CONV_K - 1):]

    @pl.when(step >= n_prompt_steps)
    def _sample():
        rt = SAMPLE_SUBTILE
        nb = rt // dec_seq

        def mixer(s):
            rows, bat = slice(s * rt, (s + 1) * rt), slice(s * nb, (s + 1) * nb)
            x = xs_ref[rows]
            h = _rms(x, g1_ref[...]).astype(BF16)
            p, u, v, ta, tb = _input_proj(h, w_in_ref, sgug_ref)
            vs_ref[rows] = v

            stp = stp_ref[:, bat, :]
            old = jnp.concatenate([stp.reshape(POOL_PAD * nb, D_POOL), jnp.zeros((nb, D_POOL), F32)],
                                  axis=0)
            p_hi, p_lo = _split_bf16(p)
            s_hi, s_lo = _split_bf16(old)
            means = []
            for g in range(N_GROUPS):
                sl = slice(g * D_GROUP, (g + 1) * D_GROUP)
                new2 = jnp.concatenate([p_hi[:, sl], p_lo[:, sl]], axis=1)
                old2 = jnp.concatenate([s_hi[:, sl], s_lo[:, sl]], axis=1)
                r = _dot(pool_a_ref[g], new2) + _dot(pool_b_ref[g], old2)
                means.append(r[:, :D_GROUP] + r[:, D_GROUP:])
            y = _pool_groups((jnp.concatenate(means, axis=1) - p).astype(BF16), pw_ref, pscale_ref)
            p3 = p.reshape(nb, dec_seq, D_POOL)
            for i in range(POOL_PAD):
                k = i + dec_seq
                npools_ref[i, bat, :] = stp[k] if k < POOL_PAD else p3[:, k - POOL_PAD, :]

            vb = v.astype(BF16)
            mixes = []
            for hh in range(N_HEADS):
                bias = jnp.tile(bias_ref[0:dec_seq, hh * D_HEAD:(hh + 1) * D_HEAD], (nb, 1))
                mixes.append(_dot(wsb_ref[hh], vb[:, hh * D_HEAD:(hh + 1) * D_HEAD]) + bias)
            mix = jnp.concatenate(mixes, axis=1)
            return _mixer_merge(x, ta, tb, y, mix, u, wpo_ref, wso_ref, wo_ref)

        tok = lax.broadcasted_iota(jnp.int32, (nb, dec_seq, FF_CHUNK), 1)

        def ffn(s, x1):
            rows, bat = slice(s * rt, (s + 1) * rt), slice(s * nb, (s + 1) * nb)

            def shifted(j, a):
                lo, hi = j * FF_CHUNK, (j + 1) * FF_CHUNK
                a3 = a.reshape(nb, dec_seq, FF_CHUNK)
                back = stc_ref[0, bat, :, lo:hi]
                back2, back1 = back[:, 0:1, :], back[:, 1:2, :]
                s1 = jnp.where(tok == 0, back1, pltpu.roll(a3, 1, axis=1))
                s2 = jnp.where(tok == 0, back2, jnp.where(tok == 1, back1, pltpu.roll(a3, 2, axis=1)))
                return s1.reshape(rt, FF_CHUNK), s2.reshape(rt, FF_CHUNK)

            y_out, a_parts = conv_ffn(x1, shifted)
            ys_ref[rows] = y_out
            for j, a in enumerate(a_parts):
                a3 = a.reshape(nb, dec_seq, FF_CHUNK)
                nconvs_ref[0, bat, :, j * FF_CHUNK:(j + 1) * FF_CHUNK] = a3[:, dec_seq - (CONV_K - 1):, :]

        n_sub = SAMPLE_TILE // rt
        x1s = [mixer(s) for s in range(n_sub)]
        for s in range(n_sub):
            ffn(s, x1s[s])


def _pool_matrices(dec_seq, n_batch):
    a = np.zeros((N_GROUPS, n_batch * dec_seq, n_batch * dec_seq), np.float32)
    b = np.zeros((N_GROUPS, n_batch * dec_seq, POOL_HIST * n_batch), np.float32)
    for g, w in enumerate(POOL_WINDOWS):
        for bb in range(n_batch):
            for t in range(dec_seq):
                for k in range(w):
                    i = POOL_PAD + t - k
                    if i >= POOL_PAD:
                        a[g, bb * dec_seq + t, bb * dec_seq + i - POOL_PAD] = 1.0 / w
                    else:
                        b[g, bb * dec_seq + t, i * n_batch + bb] = 1.0 / w
    return jnp.asarray(a, BF16), jnp.asarray(b, BF16)


def _first_rows_table():
    t = np.arange(POOL_HIST, dtype=np.float32)[:, None]
    w = np.repeat(np.asarray(POOL_WINDOWS, np.float32), D_GROUP)[None, :]
    return jnp.asarray(1.0 / np.minimum(w, t + 1.0), F32), jnp.asarray(1.0 / w, F32)


def _whole(shape):
    zeros = (0,) * len(shape)
    return pl.BlockSpec(shape, lambda i: zeros, pipeline_mode=pl.Buffered(1))


def kernel(x_prompt, x_sample, state_pool, state_ffn_conv, norm1_g, w_in, pool_w, pool_scale,
           w_pool_out, sgu_norm_g, sgu_w, sgu_b, w_sgu_out, w_o, norm2_g, ffn_w_up, ffn_w_gate,
           ffn_conv_w, ffn_conv_b, ffn_w_down, final_norm_g):
    depth = norm1_g.shape[0]
    assert depth == 1
    batch, seq, _ = x_prompt.shape
    dec_batch, dec_seq, _ = x_sample.shape
    assert seq % PROMPT_TILE == 0 and PROMPT_SUBTILE % CHUNK == 0 and PROMPT_SUBTILE >= 2 * POOL_HIST
    assert SAMPLE_SUBTILE % dec_seq == 0 and (dec_batch * dec_seq) % SAMPLE_TILE == 0
    assert CONV_K - 1 <= dec_seq <= CHUNK and dec_seq % SUBLANES == 0
    assert sgu_w.shape[-1] == CHUNK and state_pool.shape[2] == POOL_PAD

    tiles_per_seq = seq // PROMPT_TILE
    n_p = batch * tiles_per_seq
    rows = dec_batch * dec_seq
    n_s = rows // SAMPLE_TILE
    tile_batch = SAMPLE_TILE // dec_seq

    tbl, invw = _first_rows_table()
    pool_a, pool_b = _pool_matrices(dec_seq, SAMPLE_SUBTILE // dec_seq)
    onehot = np.zeros((CHUNK, SAMPLE_SUBTILE), np.float32)
    onehot[np.arange(SAMPLE_SUBTILE) % dec_seq, np.arange(SAMPLE_SUBTILE)] = 1.0
    onehot = jnp.asarray(onehot, BF16)
    wscale = jnp.asarray(np.where(np.arange(D_IN) < D_POOL, 1.0, 0.5)[None, :], F32)

    p_tile = lambda i: jnp.minimum(i, n_p - 1)
    s_tile = lambda i: jnp.maximum(i - n_p, 0)
    s_rows = lambda n: pl.BlockSpec((SAMPLE_TILE, n), lambda i: (s_tile(i), 0), pipeline_mode=pl.Buffered(1))
    hbm = pl.BlockSpec(memory_space=pl.ANY)

    s_pool = pl.BlockSpec((POOL_PAD, tile_batch, D_POOL), lambda i: (0, s_tile(i), 0),
                          pipeline_mode=pl.Buffered(1))
    s_conv = pl.BlockSpec((1, tile_batch, CONV_K - 1, D_FF), lambda i: (0, s_tile(i), 0, 0),
                          pipeline_mode=pl.Buffered(1))
    tiled = [x_prompt, x_sample.reshape(rows, D_MODEL), jnp.transpose(state_pool[0], (1, 0, 2)),
             state_ffn_conv]
    tiled_specs = [
        pl.BlockSpec((1, PROMPT_TILE, D_MODEL), lambda i: (p_tile(i) // tiles_per_seq, p_tile(i) % tiles_per_seq, 0)),
        s_rows(D_MODEL), s_pool, s_conv]
    small = [norm1_g, pool_scale, sgu_norm_g, sgu_b, norm2_g, jnp.transpose(ffn_conv_w, (1, 0, 2)), ffn_conv_b,
             final_norm_g.reshape(1, D_MODEL), sgu_w, pool_w, invw, tbl, pool_a, pool_b, onehot, wscale]
    big = [w_in, w_pool_out, w_sgu_out, w_o, ffn_w_up, ffn_w_gate, ffn_w_down]

    out_shape = [
        jax.ShapeDtypeStruct((batch, seq, D_MODEL), F32),
        jax.ShapeDtypeStruct((rows, D_MODEL), F32),
        jax.ShapeDtypeStruct((POOL_PAD, batch, D_POOL), F32),
        jax.ShapeDtypeStruct((1, batch, CONV_K - 1, D_FF), F32),
        jax.ShapeDtypeStruct((POOL_PAD, dec_batch, D_POOL), F32),
        jax.ShapeDtypeStruct((1, dec_batch, CONV_K - 1, D_FF), F32),
        jax.ShapeDtypeStruct((rows, D_SGU), F32)]
    out_specs = [
        pl.BlockSpec((1, PROMPT_TILE, D_MODEL), lambda i: (p_tile(i) // tiles_per_seq, p_tile(i) % tiles_per_seq, 0)),
        s_rows(D_MODEL),
        pl.BlockSpec((POOL_PAD, batch, D_POOL), lambda i: (0, 0, 0)),
        pl.BlockSpec((1, 1, CONV_K - 1, D_FF), lambda i: (0, p_tile(i) // tiles_per_seq, 0, 0)),
        s_pool, s_conv, s_rows(D_SGU)]
    scratch = [
        pltpu.VMEM((D_MODEL, D_IN), BF16), pltpu.VMEM((D_POOL, D_MODEL), BF16), pltpu.VMEM((D_SGU, D_MODEL), BF16),
        pltpu.VMEM((D_MODEL, D_MODEL), BF16), pltpu.VMEM((D_MODEL, D_FF), BF16), pltpu.VMEM((D_MODEL, D_FF), BF16),
        pltpu.VMEM((D_FF, D_MODEL), BF16),
        pltpu.VMEM((N_GROUPS // 2, 2 * D_GROUP, 2 * D_GROUP), BF16),
        pltpu.VMEM((N_HEADS, CHUNK, CHUNK), BF16),
        pltpu.VMEM((CHUNK, D_SGU), F32),
        pltpu.VMEM((POOL_HIST, D_POOL), F32), pltpu.VMEM((SUBLANES, D_FF), F32),
        pltpu.VMEM((CONV_K, 1, D_FF), F32), pltpu.VMEM((1, D_FF), F32),
        pltpu.VMEM((N_HEADS, SAMPLE_SUBTILE, SAMPLE_SUBTILE), BF16)]

    outs = pl.pallas_call(
        functools.partial(_layer_kernel, n_prompt_steps=n_p, tiles_per_seq=tiles_per_seq, dec_seq=dec_seq),
        grid=(n_p + n_s,),
        in_specs=tiled_specs + [_whole(a.shape) for a in small] + [hbm] * len(big),
        out_specs=out_specs,
        out_shape=out_shape,
        scratch_shapes=scratch,
        compiler_params=pltpu.CompilerParams(dimension_semantics=("arbitrary",),
                                             vmem_limit_bytes=VMEM_LIMIT_BYTES),
        name="layer_step",
    )(*tiled, *small, *big)
    y_prompt, y_s, npool_p, nconv_p, npool_s, nconv_s, v_s = outs
    return (y_prompt, y_s.reshape(dec_batch, dec_seq, D_MODEL),
            jnp.transpose(npool_p, (1, 0, 2))[None], jnp.transpose(npool_s, (1, 0, 2))[None],
            nconv_p, nconv_s,
            v_s.reshape(1, dec_batch, dec_seq, D_SGU))
```

```python
import functools

import numpy as np
import jax
import jax.numpy as jnp
from jax import lax
from jax.experimental import pallas as pl
from jax.experimental.pallas import tpu as pltpu

D_MODEL = 1024
POOL_WINDOWS = (2, 4, 8, 16)
N_GROUPS = 4
D_GROUP = 128
D_POOL = N_GROUPS * D_GROUP
POOL_PAD = max(POOL_WINDOWS) - 1
POOL_HIST = POOL_PAD + 1
CHUNK = 128
N_HEADS = 4
D_SGU = 512
D_HEAD = D_SGU // N_HEADS
D_IN = D_POOL + 2 * D_SGU + 2 * D_MODEL
_GATE_A = D_POOL + 2 * D_SGU
_GATE_B = _GATE_A + D_MODEL
D_FF = 2816
CONV_K = 3
EPS = 1e-6

SUBLANES = 8
LANES = 128
FF_CHUNK = 256
N_FF_CHUNKS = D_FF // FF_CHUNK
PROMPT_SUBTILE = 256
PROMPT_TILE = 2 * PROMPT_SUBTILE
SAMPLE_SUBTILE = 256
SAMPLE_TILE = 2 * SAMPLE_SUBTILE
WIDE_ROWS = 128
NARROW_ROWS = 256
LOAD_SLOTS = 4
CAST_ROWS = 16
VMEM_LIMIT_BYTES = 60 * 1024 * 1024

_GELU_C = 0.7978845608028654
_GELU_C3 = _GELU_C * 0.044715

BF16 = jnp.bfloat16
F32 = jnp.float32


def _dot(a, b):
    return jnp.dot(a, b, preferred_element_type=F32)


def _rms(x, g):
    ms = jnp.mean(x * x, axis=-1, keepdims=True)
    return (x * lax.rsqrt(ms + EPS)) * g


def _gelu_of_twice(hx):
    u = hx * (2.0 * _GELU_C + (8.0 * _GELU_C3) * (hx * hx))
    return hx + hx * jnp.tanh(u)


def _split_bf16(x):
    hi = x.astype(BF16)
    lo = (x - hi.astype(F32)).astype(BF16)
    return hi, lo


def _input_proj(h, w_in_ref, sgug_ref):
    p = _dot(h, w_in_ref[:, 0:D_POOL])
    u = _gelu_of_twice(_dot(h, w_in_ref[:, D_POOL:D_POOL + D_SGU]))
    v = _rms(_gelu_of_twice(_dot(h, w_in_ref[:, D_POOL + D_SGU:D_POOL + 2 * D_SGU])), sgug_ref[...])
    ta = jnp.tanh(_dot(h, w_in_ref[:, _GATE_A:_GATE_B]))
    tb = jnp.tanh(_dot(h, w_in_ref[:, _GATE_B:D_IN]))
    return p, u, v, ta, tb


def _pool_groups(d, pw_ref, pscale_ref):
    y01 = _dot(d[:, 0:2 * D_GROUP], pw_ref[0])
    y23 = _dot(d[:, 2 * D_GROUP:], pw_ref[1])
    return (jnp.concatenate([y01, y23], axis=1) * pscale_ref[...]).astype(BF16)


def _mixer_merge(x, ta, tb, y, mix, u, wpo_ref, wso_ref, wo_ref):
    a_out = _dot(y, wpo_ref[...])
    b_out = _dot((u * mix).astype(BF16), wso_ref[...])
    twice_m = ((ta * a_out + a_out) + (tb * b_out + b_out)).astype(BF16)
    return x + _dot(twice_m, wo_ref[...])


def _load_weight_group(weights, chunk_rows, max_cols):
    chunks = [(hbm, vmem, r0, cols, scale)
              for hbm, vmem, rows, cols, scale in weights for r0 in range(0, rows, chunk_rows)]
    lookahead = LOAD_SLOTS - 1

    def run(stage, sem):
        def copy(g):
            hbm, _, r0, cols, _ = chunks[g]
            slot = g % LOAD_SLOTS
            return pltpu.make_async_copy(hbm.at[0, r0:r0 + chunk_rows, :],
                                         stage.at[slot, :, 0:cols], sem.at[slot])

        for g in range(min(lookahead, len(chunks))):
            copy(g).start()
        for g, (_, vmem, r0, cols, scale) in enumerate(chunks):
            if g + lookahead < len(chunks):
                copy(g + lookahead).start()
            copy(g).wait()
            slot = g % LOAD_SLOTS

            def cast(r, carry, vmem=vmem, r0=r0, cols=cols, slot=slot, scale=scale):
                rr = pl.multiple_of(r * CAST_ROWS, CAST_ROWS)
                w = stage[slot, pl.ds(rr, CAST_ROWS), 0:cols]
                if scale is not None:
                    w = w * (scale if isinstance(scale, float) else scale[...])
                vmem[pl.ds(r0 + rr, CAST_ROWS), :] = w.astype(BF16)
                return carry

            lax.fori_loop(0, chunk_rows // CAST_ROWS, cast, 0)

    pl.run_scoped(run, pltpu.VMEM((LOAD_SLOTS, chunk_rows, max_cols), F32),
                  pltpu.SemaphoreType.DMA((LOAD_SLOTS,)))


def _layer_kernel(
        xp_ref, xs_ref, stp_ref, stc_ref,
        g1_ref, pscale_ref, sgug_ref, sgub_ref, g2_ref, cw_ref, cb_ref, gf_ref, sguw_ref, poolw_ref,
        invw_ref, tbl_ref, pool_a_ref, pool_b_ref, onehot_ref, wscale_ref,
        w_in_hbm, wpo_hbm, wso_hbm, wo_hbm, wup_hbm, wgate_hbm, wdown_hbm,
        yp_ref, ys_ref, npoolp_ref, nconvp_ref, npools_ref, nconvs_ref, vs_ref,
        w_in_ref, wpo_ref, wso_ref, wo_ref, wup_ref, wgate_ref, wdown_ref, pw_ref, wsm_ref, bias_ref,
        carry_p, carry_a, hcw_ref, hcb_ref, wsb_ref,
        *, n_prompt_steps, tiles_per_seq, dec_seq):
    step = pl.program_id(0)

    @pl.when(step == 0)
    def _prepare():
        _load_weight_group(
            [(w_in_hbm, w_in_ref, D_MODEL, D_IN, wscale_ref), (wup_hbm, wup_ref, D_MODEL, D_FF, None),
             (wgate_hbm, wgate_ref, D_MODEL, D_FF, None)], WIDE_ROWS, D_IN)
        _load_weight_group(
            [(wdown_hbm, wdown_ref, D_FF, D_MODEL, None), (wo_hbm, wo_ref, D_MODEL, D_MODEL, 0.5),
             (wpo_hbm, wpo_ref, D_POOL, D_MODEL, None), (wso_hbm, wso_ref, D_SGU, D_MODEL, None)],
            NARROW_ROWS, D_MODEL)
        hcw_ref[...] = 0.5 * cw_ref[...]
        hcb_ref[...] = 0.5 * cb_ref[...]
        pw_ref[...] = jnp.zeros_like(pw_ref)
        for g in range(N_GROUPS):
            lo = (g % 2) * D_GROUP
            pw_ref[g // 2, lo:lo + D_GROUP, lo:lo + D_GROUP] = poolw_ref[0, g].astype(BF16)
        row = lax.broadcasted_iota(jnp.int32, (CHUNK, CHUNK), 0)
        col = lax.broadcasted_iota(jnp.int32, (CHUNK, CHUNK), 1)
        for hh in range(N_HEADS):
            wsm_ref[hh] = jnp.where(row >= col, sguw_ref[0, hh], 0.0).astype(BF16)
            b_row = sgub_ref[0, hh:hh + 1, :]
            bias_ref[:, hh * D_HEAD:(hh + 1) * D_HEAD] = jnp.broadcast_to(b_row, (D_HEAD, CHUNK)).T
        rt = SAMPLE_SUBTILE
        row = lax.broadcasted_iota(jnp.int32, (rt, rt), 0)
        col = lax.broadcasted_iota(jnp.int32, (rt, rt), 1)
        keep = ((row // dec_seq) == (col // dec_seq)) & (row >= col)
        for hh in range(N_HEADS):
            w_rows = jnp.tile(sguw_ref[0, hh, 0:dec_seq, :], (rt // dec_seq, 1)).astype(BF16)
            wsb_ref[hh] = jnp.where(keep, _dot(w_rows, onehot_ref[...]), 0.0).astype(BF16)

    def conv_ffn(x1, shifted):
        h2 = _rms(x1, g2_ref[...]).astype(BF16)
        f_parts, a_parts = [], []
        for j in range(N_FF_CHUNKS):
            lo, hi = j * FF_CHUNK, (j + 1) * FF_CHUNK
            a = _dot(h2, wup_ref[:, lo:hi])
            s1, s2 = shifted(j, a)
            half_c = (s2 * hcw_ref[0, :, lo:hi] + s1 * hcw_ref[1, :, lo:hi] + a * hcw_ref[2, :, lo:hi]
                      + hcb_ref[:, lo:hi])
            a_parts.append(a)
            f_parts.append((_gelu_of_twice(half_c) * _dot(h2, wgate_ref[:, lo:hi])).astype(BF16))
        x2 = x1 + _dot(jnp.concatenate(f_parts, axis=1), wdown_ref[...])
        return _rms(x2, gf_ref[...]), a_parts

    @pl.when(step < n_prompt_steps)
    def _prompt():
        st = PROMPT_SUBTILE
        n_sub = PROMPT_TILE // st
        first_tile = (step % tiles_per_seq) == 0

        @pl.when(first_tile)
        def _():
            carry_p[...] = jnp.zeros_like(carry_p)
            carry_a[...] = jnp.zeros_like(carry_a)

        def mixer(x, p_hist, first):
            h = _rms(x, g1_ref[...]).astype(BF16)
            p, u, v, ta, tb = _input_proj(h, w_in_ref, sgug_ref)

            level = jnp.concatenate([p_hist, p], axis=0)
            sums = []
            for g, w in enumerate(POOL_WINDOWS):
                level = level + pltpu.roll(level, w // 2, axis=0)
                sums.append(level[:, 0:D_GROUP])
                if g + 1 < N_GROUPS:
                    level = level[:, D_GROUP:]
            win = jnp.concatenate(sums, axis=1)[POOL_HIST:]
            scaled = jnp.concatenate([win[:POOL_HIST] * first, win[POOL_HIST:] * invw_ref[...]], axis=0)
            y = _pool_groups((scaled - p).astype(BF16), pw_ref, pscale_ref)

            vb = v.astype(BF16)
            n_chunks = st // CHUNK
            per_head = []
            for hh in range(N_HEADS):
                rhs = jnp.concatenate(
                    [vb[c * CHUNK:(c + 1) * CHUNK, hh * D_HEAD:(hh + 1) * D_HEAD] for c in range(n_chunks)],
                    axis=1)
                res = _dot(wsm_ref[hh], rhs)
                bias = bias_ref[:, hh * D_HEAD:(hh + 1) * D_HEAD]
                per_head.append([res[:, c * D_HEAD:(c + 1) * D_HEAD] + bias for c in range(n_chunks)])
            mix = jnp.concatenate(
                [jnp.concatenate([per_head[hh][c] for hh in range(N_HEADS)], axis=1) for c in range(n_chunks)],
                axis=0)
            return _mixer_merge(x, ta, tb, y, mix, u, wpo_ref, wso_ref, wo_ref), p[st - POOL_HIST:]

        invw_rows = jnp.broadcast_to(invw_ref[...], tbl_ref.shape)
        p_hist = carry_p[...]
        x1s = []
        for s in range(n_sub):
            first = jnp.where(first_tile, tbl_ref[...], invw_rows) if s == 0 else invw_rows
            x1, p_hist = mixer(xp_ref[0, s * st:(s + 1) * st], p_hist, first)
            x1s.append(x1)
        carry_p[...] = p_hist
        npoolp_ref[:, pl.ds(step // tiles_per_seq, 1), :] = p_hist[POOL_HIST - POOL_PAD:][:, None, :]

        a_hist = [carry_a[:, j * FF_CHUNK:(j + 1) * FF_CHUNK] for j in range(N_FF_CHUNKS)]

        def shifted(j, a):
            ext = jnp.concatenate([a_hist[j], a], axis=0)
            return pltpu.roll(ext, 1, axis=0)[SUBLANES:], pltpu.roll(ext, 2, axis=0)[SUBLANES:]

        y_out, a_parts = conv_ffn(jnp.concatenate(x1s, axis=0), shifted)
        a_hist = [a[PROMPT_TILE - SUBLANES:] for a in a_parts]
        yp_ref[0] = y_out
        for j in range(N_FF_CHUNKS):
            carry_a[:, j * FF_CHUNK:(j + 1) * FF_CHUNK] = a_hist[j]
            nconvp_ref[0, 0, :, j * FF_CHUNK:(j + 1) * FF_CHUNK] = a_hist[j][SUBLANES - (CONV_K - 1):]

    @pl.when(step >= n_prompt_steps)
    def _sample():
        rt = SAMPLE_SUBTILE
        nb = rt // dec_seq

        def mixer(s):
            rows, bat = slice(s * rt, (s + 1) * rt), slice(s * nb, (s + 1) * nb)
            x = xs_ref[rows]
            h = _rms(x, g1_ref[...]).astype(BF16)
            p, u, v, ta, tb = _input_proj(h, w_in_ref, sgug_ref)
            vs_ref[rows] = v

            stp = stp_ref[:, bat, :]
            old = jnp.concatenate([stp.reshape(POOL_PAD * nb, D_POOL), jnp.zeros((nb, D_POOL), F32)],
                                  axis=0)
            p_hi, p_lo = _split_bf16(p)
            s_hi, s_lo = _split_bf16(old)
            means = []
            for g in range(N_GROUPS):
                sl = slice(g * D_GROUP, (g + 1) * D_GROUP)
                new2 = jnp.concatenate([p_hi[:, sl], p_lo[:, sl]], axis=1)
                old2 = jnp.concatenate([s_hi[:, sl], s_lo[:, sl]], axis=1)
                r = _dot(pool_a_ref[g], new2) + _dot(pool_b_ref[g], old2)
                means.append(r[:, :D_GROUP] + r[:, D_GROUP:])
            y = _pool_groups((jnp.concatenate(means, axis=1) - p).astype(BF16), pw_ref, pscale_ref)
            p3 = p.reshape(nb, dec_seq, D_POOL)
            for i in range(POOL_PAD):
                k = i + dec_seq
                npools_ref[i, bat, :] = stp[k] if k < POOL_PAD else p3[:, k - POOL_PAD, :]

            vb = v.astype(BF16)
            mixes = []
            for hh in range(N_HEADS):
                bias = jnp.tile(bias_ref[0:dec_seq, hh * D_HEAD:(hh + 1) * D_HEAD], (nb, 1))
                mixes.append(_dot(wsb_ref[hh], vb[:, hh * D_HEAD:(hh + 1) * D_HEAD]) + bias)
            mix = jnp.concatenate(mixes, axis=1)
            return _mixer_merge(x, ta, tb, y, mix, u, wpo_ref, wso_ref, wo_ref)

        tok = lax.broadcasted_iota(jnp.int32, (nb, dec_seq, FF_CHUNK), 1)

        def ffn(s, x1):
            rows, bat = slice(s * rt, (s + 1) * rt), slice(s * nb, (s + 1) * nb)

            def shifted(j, a):
                lo, hi = j * FF_CHUNK, (j + 1) * FF_CHUNK
                a3 = a.reshape(nb, dec_seq, FF_CHUNK)
                back = stc_ref[0, bat, :, lo:hi]
                back2, back1 = back[:, 0:1, :], back[:, 1:2, :]
                s1 = jnp.where(tok == 0, back1, pltpu.roll(a3, 1, axis=1))
                s2 = jnp.where(tok == 0, back2, jnp.where(tok == 1, back1, pltpu.roll(a3, 2, axis=1)))
                return s1.reshape(rt, FF_CHUNK), s2.reshape(rt, FF_CHUNK)

            y_out, a_parts = conv_ffn(x1, shifted)
            ys_ref[rows] = y_out
            for j, a in enumerate(a_parts):
                a3 = a.reshape(nb, dec_seq, FF_CHUNK)
                nconvs_ref[0, bat, :, j * FF_CHUNK:(j + 1) * FF_CHUNK] = a3[:, dec_seq - (CONV_K - 1):, :]

        n_sub = SAMPLE_TILE // rt
        x1s = [mixer(s) for s in range(n_sub)]
        for s in range(n_sub):
            ffn(s, x1s[s])


def _pool_matrices(dec_seq, n_batch):
    a = np.zeros((N_GROUPS, n_batch * dec_seq, n_batch * dec_seq), np.float32)
    b = np.zeros((N_GROUPS, n_batch * dec_seq, POOL_HIST * n_batch), np.float32)
    for g, w in enumerate(POOL_WINDOWS):
        for bb in range(n_batch):
            for t in range(dec_seq):
                for k in range(w):
                    i = POOL_PAD + t - k
                    if i >= POOL_PAD:
                        a[g, bb * dec_seq + t, bb * dec_seq + i - POOL_PAD] = 1.0 / w
                    else:
                        b[g, bb * dec_seq + t, i * n_batch + bb] = 1.0 / w
    return jnp.asarray(a, BF16), jnp.asarray(b, BF16)


def _first_rows_table():
    t = np.arange(POOL_HIST, dtype=np.float32)[:, None]
    w = np.repeat(np.asarray(POOL_WINDOWS, np.float32), D_GROUP)[None, :]
    return jnp.asarray(1.0 / np.minimum(w, t + 1.0), F32), jnp.asarray(1.0 / w, F32)


def _whole(shape):
    zeros = (0,) * len(shape)
    return pl.BlockSpec(shape, lambda i: zeros, pipeline_mode=pl.Buffered(1))


def kernel(x_prompt, x_sample, state_pool, state_ffn_conv, norm1_g, w_in, pool_w, pool_scale,
           w_pool_out, sgu_norm_g, sgu_w, sgu_b, w_sgu_out, w_o, norm2_g, ffn_w_up, ffn_w_gate,
           ffn_conv_w, ffn_conv_b, ffn_w_down, final_norm_g):
    depth = norm1_g.shape[0]
    assert depth == 1
    batch, seq, _ = x_prompt.shape
    dec_batch, dec_seq, _ = x_sample.shape
    assert seq % PROMPT_TILE == 0 and PROMPT_SUBTILE % CHUNK == 0 and PROMPT_SUBTILE >= 2 * POOL_HIST
    assert SAMPLE_SUBTILE % dec_seq == 0 and (dec_batch * dec_seq) % SAMPLE_TILE == 0
    assert CONV_K - 1 <= dec_seq <= CHUNK and dec_seq % SUBLANES == 0
    assert sgu_w.shape[-1] == CHUNK and state_pool.shape[2] == POOL_PAD

    tiles_per_seq = seq // PROMPT_TILE
    n_p = batch * tiles_per_seq
    rows = dec_batch * dec_seq
    n_s = rows // SAMPLE_TILE
    tile_batch = SAMPLE_TILE // dec_seq

    tbl, invw = _first_rows_table()
    pool_a, pool_b = _pool_matrices(dec_seq, SAMPLE_SUBTILE // dec_seq)
    onehot = np.zeros((CHUNK, SAMPLE_SUBTILE), np.float32)
    onehot[np.arange(SAMPLE_SUBTILE) % dec_seq, np.arange(SAMPLE_SUBTILE)] = 1.0
    onehot = jnp.asarray(onehot, BF16)
    wscale = jnp.asarray(np.where(np.arange(D_IN) < D_POOL, 1.0, 0.5)[None, :], F32)

    p_tile = lambda i: jnp.minimum(i, n_p - 1)
    s_tile = lambda i: jnp.maximum(i - n_p, 0)
    s_rows = lambda n: pl.BlockSpec((SAMPLE_TILE, n), lambda i: (s_tile(i), 0), pipeline_mode=pl.Buffered(1))
    hbm = pl.BlockSpec(memory_space=pl.ANY)

    s_pool = pl.BlockSpec((POOL_PAD, tile_batch, D_POOL), lambda i: (0, s_tile(i), 0),
                          pipeline_mode=pl.Buffered(1))
    s_conv = pl.BlockSpec((1, tile_batch, CONV_K - 1, D_FF), lambda i: (0, s_tile(i), 0, 0),
                          pipeline_mode=pl.Buffered(1))
    tiled = [x_prompt, x_sample.reshape(rows, D_MODEL), jnp.transpose(state_pool[0], (1, 0, 2)),
             state_ffn_conv]
    tiled_specs = [
        pl.BlockSpec((1, PROMPT_TILE, D_MODEL), lambda i: (p_tile(i) // tiles_per_seq, p_tile(i) % tiles_per_seq, 0)),
        s_rows(D_MODEL), s_pool, s_conv]
    small = [norm1_g, pool_scale, sgu_norm_g, sgu_b, norm2_g, jnp.transpose(ffn_conv_w, (1, 0, 2)), ffn_conv_b,
             final_norm_g.reshape(1, D_MODEL), sgu_w, pool_w, invw, tbl, pool_a, pool_b, onehot, wscale]
    big = [w_in, w_pool_out, w_sgu_out, w_o, ffn_w_up, ffn_w_gate, ffn_w_down]

    out_shape = [
        jax.ShapeDtypeStruct((batch, seq, D_MODEL), F32),
        jax.ShapeDtypeStruct((rows, D_MODEL), F32),
        jax.ShapeDtypeStruct((POOL_PAD, batch, D_POOL), F32),
        jax.ShapeDtypeStruct((1, batch, CONV_K - 1, D_FF), F32),
        jax.ShapeDtypeStruct((POOL_PAD, dec_batch, D_POOL), F32),
        jax.ShapeDtypeStruct((1, dec_batch, CONV_K - 1, D_FF), F32),
        jax.ShapeDtypeStruct((rows, D_SGU), F32)]
    out_specs = [
        pl.BlockSpec((1, PROMPT_TILE, D_MODEL), lambda i: (p_tile(i) // tiles_per_seq, p_tile(i) % tiles_per_seq, 0)),
        s_rows(D_MODEL),
        pl.BlockSpec((POOL_PAD, batch, D_POOL), lambda i: (0, 0, 0)),
        pl.BlockSpec((1, 1, CONV_K - 1, D_FF), lambda i: (0, p_tile(i) // tiles_per_seq, 0, 0)),
        s_pool, s_conv, s_rows(D_SGU)]
    scratch = [
        pltpu.VMEM((D_MODEL, D_IN), BF16), pltpu.VMEM((D_POOL, D_MODEL), BF16), pltpu.VMEM((D_SGU, D_MODEL), BF16),
        pltpu.VMEM((D_MODEL, D_MODEL), BF16), pltpu.VMEM((D_MODEL, D_FF), BF16), pltpu.VMEM((D_MODEL, D_FF), BF16),
        pltpu.VMEM((D_FF, D_MODEL), BF16),
        pltpu.VMEM((N_GROUPS // 2, 2 * D_GROUP, 2 * D_GROUP), BF16),
        pltpu.VMEM((N_HEADS, CHUNK, CHUNK), BF16),
        pltpu.VMEM((CHUNK, D_SGU), F32),
        pltpu.VMEM((POOL_HIST, D_POOL), F32), pltpu.VMEM((SUBLANES, D_FF), F32),
        pltpu.VMEM((CONV_K, 1, D_FF), F32), pltpu.VMEM((1, D_FF), F32),
        pltpu.VMEM((N_HEADS, SAMPLE_SUBTILE, SAMPLE_SUBTILE), BF16)]

    outs = pl.pallas_call(
        functools.partial(_layer_kernel, n_prompt_steps=n_p, tiles_per_seq=tiles_per_seq, dec_seq=dec_seq),
        grid=(n_p + n_s,),
        in_specs=tiled_specs + [_whole(a.shape) for a in small] + [hbm] * len(big),
        out_specs=out_specs,
        out_shape=out_shape,
        scratch_shapes=scratch,
        compiler_params=pltpu.CompilerParams(dimension_semantics=("arbitrary",),
                                             vmem_limit_bytes=VMEM_LIMIT_BYTES),
        name="layer_step",
    )(*tiled, *small, *big)
    y_prompt, y_s, npool_p, nconv_p, npool_s, nconv_s, v_s = outs
    return (y_prompt, y_s.reshape(dec_batch, dec_seq, D_MODEL),
            jnp.transpose(npool_p, (1, 0, 2))[None], jnp.transpose(npool_s, (1, 0, 2))[None],
            nconv_p, nconv_s,
            v_s.reshape(1, dec_batch, dec_seq, D_SGU))
```

```python
import functools

import numpy as np
import jax
import jax.numpy as jnp
from jax import lax
from jax.experimental import pallas as pl
from jax.experimental.pallas import tpu as pltpu

D_MODEL = 1024
POOL_WINDOWS = (2, 4, 8, 16)
N_GROUPS = 4
D_GROUP = 128
D_POOL = N_GROUPS * D_GROUP
POOL_PAD = max(POOL_WINDOWS) - 1
POOL_HIST = POOL_PAD + 1
CHUNK = 128
N_HEADS = 4
D_SGU = 512
D_HEAD = D_SGU // N_HEADS
D_IN = D_POOL + 2 * D_SGU + 2 * D_MODEL
_GATE_A = D_POOL + 2 * D_SGU
_GATE_B = _GATE_A + D_MODEL
D_FF = 2816
CONV_K = 3
EPS = 1e-6

SUBLANES = 8
LANES = 128
FF_CHUNK = 256
N_FF_CHUNKS = D_FF // FF_CHUNK
PROMPT_SUBTILE = 256
PROMPT_TILE = 2 * PROMPT_SUBTILE
SAMPLE_SUBTILE = 256
SAMPLE_TILE = 2 * SAMPLE_SUBTILE
WIDE_ROWS = 128
NARROW_ROWS = 256
LOAD_SLOTS = 4
CAST_ROWS = 16
VMEM_LIMIT_BYTES = 60 * 1024 * 1024

_GELU_C = 0.7978845608028654
_GELU_C3 = _GELU_C * 0.044715

BF16 = jnp.bfloat16
F32 = jnp.float32


def _dot(a, b):
    return jnp.dot(a, b, preferred_element_type=F32)


def _rms(x, g):
    ms = jnp.mean(x * x, axis=-1, keepdims=True)
    return (x * lax.rsqrt(ms + EPS)) * g


def _gelu_of_twice(hx):
    u = hx * (2.0 * _GELU_C + (8.0 * _GELU_C3) * (hx * hx))
    return hx + hx * jnp.tanh(u)


def _split_bf16(x):
    hi = x.astype(BF16)
    lo = (x - hi.astype(F32)).astype(BF16)
    return hi, lo


def _input_proj(h, w_in_ref, sgug_ref):
    p = _dot(h, w_in_ref[:, 0:D_POOL])
    u = _gelu_of_twice(_dot(h, w_in_ref[:, D_POOL:D_POOL + D_SGU]))
    v = _rms(_gelu_of_twice(_dot(h, w_in_ref[:, D_POOL + D_SGU:D_POOL + 2 * D_SGU])), sgug_ref[...])
    ta = jnp.tanh(_dot(h, w_in_ref[:, _GATE_A:_GATE_B]))
    tb = jnp.tanh(_dot(h, w_in_ref[:, _GATE_B:D_IN]))
    return p, u, v, ta, tb


def _pool_groups(d, pw_ref, pscale_ref):
    y01 = _dot(d[:, 0:2 * D_GROUP], pw_ref[0])
    y23 = _dot(d[:, 2 * D_GROUP:], pw_ref[1])
    return (jnp.concatenate([y01, y23], axis=1) * pscale_ref[...]).astype(BF16)


def _mixer_merge(x, ta, tb, y, mix, u, wpo_ref, wso_ref, wo_ref):
    a_out = _dot(y, wpo_ref[...])
    b_out = _dot((u * mix).astype(BF16), wso_ref[...])
    twice_m = ((ta * a_out + a_out) + (tb * b_out + b_out)).astype(BF16)
    return x + _dot(twice_m, wo_ref[...])


def _load_weight_group(weights, chunk_rows, max_cols):
    chunks = [(hbm, vmem, r0, cols, scale)
              for hbm, vmem, rows, cols, scale in weights for r0 in range(0, rows, chunk_rows)]
    lookahead = LOAD_SLOTS - 1

    def run(stage, sem):
        def copy(g):
            hbm, _, r0, cols, _ = chunks[g]
            slot = g % LOAD_SLOTS
            return pltpu.make_async_copy(hbm.at[0, r0:r0 + chunk_rows, :],
                                         stage.at[slot, :, 0:cols], sem.at[slot])

        for g in range(min(lookahead, len(chunks))):
            copy(g).start(priority=g % 2)
        for g, (_, vmem, r0, cols, scale) in enumerate(chunks):
            if g + lookahead < len(chunks):
                copy(g + lookahead).start(priority=(g + lookahead) % 2)
            copy(g).wait()
            slot = g % LOAD_SLOTS

            def cast(r, carry, vmem=vmem, r0=r0, cols=cols, slot=slot, scale=scale):
                rr = pl.multiple_of(r * CAST_ROWS, CAST_ROWS)
                w = stage[slot, pl.ds(rr, CAST_ROWS), 0:cols]
                if scale is not None:
                    w = w * (scale if isinstance(scale, float) else scale[...])
                vmem[pl.ds(r0 + rr, CAST_ROWS), :] = w.astype(BF16)
                return carry

            lax.fori_loop(0, chunk_rows // CAST_ROWS, cast, 0)

    pl.run_scoped(run, pltpu.VMEM((LOAD_SLOTS, chunk_rows, max_cols), F32),
                  pltpu.SemaphoreType.DMA((LOAD_SLOTS,)))


def _layer_kernel(
        xp_ref, xs_ref, stp_ref, stc_ref,
        g1_ref, pscale_ref, sgug_ref, sgub_ref, g2_ref, cw_ref, cb_ref, gf_ref, sguw_ref, poolw_ref,
        invw_ref, tbl_ref, pool_a_ref, pool_b_ref, onehot_ref, wscale_ref,
        w_in_hbm, wpo_hbm, wso_hbm, wo_hbm, wup_hbm, wgate_hbm, wdown_hbm,
        yp_ref, ys_ref, npoolp_ref, nconvp_ref, npools_ref, nconvs_ref, vs_ref,
        w_in_ref, wpo_ref, wso_ref, wo_ref, wup_ref, wgate_ref, wdown_ref, pw_ref, wsm_ref, bias_ref,
        carry_p, carry_a, hcw_ref, hcb_ref, wsb_ref,
        *, n_prompt_steps, tiles_per_seq, dec_seq):
    step = pl.program_id(0)

    @pl.when(step == 0)
    def _prepare():
        _load_weight_group(
            [(w_in_hbm, w_in_ref, D_MODEL, D_IN, wscale_ref), (wup_hbm, wup_ref, D_MODEL, D_FF, None),
             (wgate_hbm, wgate_ref, D_MODEL, D_FF, None)], WIDE_ROWS, D_IN)
        _load_weight_group(
            [(wdown_hbm, wdown_ref, D_FF, D_MODEL, None), (wo_hbm, wo_ref, D_MODEL, D_MODEL, 0.5),
             (wpo_hbm, wpo_ref, D_POOL, D_MODEL, None), (wso_hbm, wso_ref, D_SGU, D_MODEL, None)],
            NARROW_ROWS, D_MODEL)
        hcw_ref[...] = 0.5 * cw_ref[...]
        hcb_ref[...] = 0.5 * cb_ref[...]
        pw_ref[...] = jnp.zeros_like(pw_ref)
        for g in range(N_GROUPS):
            lo = (g % 2) * D_GROUP
            pw_ref[g // 2, lo:lo + D_GROUP, lo:lo + D_GROUP] = poolw_ref[0, g].astype(BF16)
        row = lax.broadcasted_iota(jnp.int32, (CHUNK, CHUNK), 0)
        col = lax.broadcasted_iota(jnp.int32, (CHUNK, CHUNK), 1)
        for hh in range(N_HEADS):
            wsm_ref[hh] = jnp.where(row >= col, sguw_ref[0, hh], 0.0).astype(BF16)
            b_row = sgub_ref[0, hh:hh + 1, :]
            bias_ref[:, hh * D_HEAD:(hh + 1) * D_HEAD] = jnp.broadcast_to(b_row, (D_HEAD, CHUNK)).T
        rt = SAMPLE_SUBTILE
        row = lax.broadcasted_iota(jnp.int32, (rt, rt), 0)
        col = lax.broadcasted_iota(jnp.int32, (rt, rt), 1)
        keep = ((row // dec_seq) == (col // dec_seq)) & (row >= col)
        for hh in range(N_HEADS):
            w_rows = jnp.tile(sguw_ref[0, hh, 0:dec_seq, :], (rt // dec_seq, 1)).astype(BF16)
            wsb_ref[hh] = jnp.where(keep, _dot(w_rows, onehot_ref[...]), 0.0).astype(BF16)

    def conv_ffn(x1, shifted):
        h2 = _rms(x1, g2_ref[...]).astype(BF16)
        f_parts, a_parts = [], []
        for j in range(N_FF_CHUNKS):
            lo, hi = j * FF_CHUNK, (j + 1) * FF_CHUNK
            a = _dot(h2, wup_ref[:, lo:hi])
            s1, s2 = shifted(j, a)
            half_c = (s2 * hcw_ref[0, :, lo:hi] + s1 * hcw_ref[1, :, lo:hi] + a * hcw_ref[2, :, lo:hi]
                      + hcb_ref[:, lo:hi])
            a_parts.append(a)
            f_parts.append((_gelu_of_twice(half_c) * _dot(h2, wgate_ref[:, lo:hi])).astype(BF16))
        x2 = x1 + _dot(jnp.concatenate(f_parts, axis=1), wdown_ref[...])
        return _rms(x2, gf_ref[...]), a_parts

    @pl.when(step < n_prompt_steps)
    def _prompt():
        st = PROMPT_SUBTILE
        n_sub = PROMPT_TILE // st
        first_tile = (step % tiles_per_seq) == 0

        @pl.when(first_tile)
        def _():
            carry_p[...] = jnp.zeros_like(carry_p)
            carry_a[...] = jnp.zeros_like(carry_a)

        def mixer(x, p_hist, first):
            h = _rms(x, g1_ref[...]).astype(BF16)
            p, u, v, ta, tb = _input_proj(h, w_in_ref, sgug_ref)

            level = jnp.concatenate([p_hist, p], axis=0)
            sums = []
            for g, w in enumerate(POOL_WINDOWS):
                level = level + pltpu.roll(level, w // 2, axis=0)
                sums.append(level[:, 0:D_GROUP])
                if g + 1 < N_GROUPS:
                    level = level[:, D_GROUP:]
            win = jnp.concatenate(sums, axis=1)[POOL_HIST:]
            scaled = jnp.concatenate([win[:POOL_HIST] * first, win[POOL_HIST:] * invw_ref[...]], axis=0)
            y = _pool_groups((scaled - p).astype(BF16), pw_ref, pscale_ref)

            vb = v.astype(BF16)
            n_chunks = st // CHUNK
            per_head = []
            for hh in range(N_HEADS):
                rhs = jnp.concatenate(
                    [vb[c * CHUNK:(c + 1) * CHUNK, hh * D_HEAD:(hh + 1) * D_HEAD] for c in range(n_chunks)],
                    axis=1)
                res = _dot(wsm_ref[hh], rhs)
                bias = bias_ref[:, hh * D_HEAD:(hh + 1) * D_HEAD]
                per_head.append([res[:, c * D_HEAD:(c + 1) * D_HEAD] + bias for c in range(n_chunks)])
            mix = jnp.concatenate(
                [jnp.concatenate([per_head[hh][c] for hh in range(N_HEADS)], axis=1) for c in range(n_chunks)],
                axis=0)
            return _mixer_merge(x, ta, tb, y, mix, u, wpo_ref, wso_ref, wo_ref), p[st - POOL_HIST:]

        invw_rows = jnp.broadcast_to(invw_ref[...], tbl_ref.shape)
        p_hist = carry_p[...]
        x1s = []
        for s in range(n_sub):
            first = jnp.where(first_tile, tbl_ref[...], invw_rows) if s == 0 else invw_rows
            x1, p_hist = mixer(xp_ref[0, s * st:(s + 1) * st], p_hist, first)
            x1s.append(x1)
        carry_p[...] = p_hist
        npoolp_ref[:, pl.ds(step // tiles_per_seq, 1), :] = p_hist[POOL_HIST - POOL_PAD:][:, None, :]

        a_hist = [carry_a[:, j * FF_CHUNK:(j + 1) * FF_CHUNK] for j in range(N_FF_CHUNKS)]

        def shifted(j, a):
            ext = jnp.concatenate([a_hist[j], a], axis=0)
            return pltpu.roll(ext, 1, axis=0)[SUBLANES:], pltpu.roll(ext, 2, axis=0)[SUBLANES:]

        y_out, a_parts = conv_ffn(jnp.concatenate(x1s, axis=0), shifted)
        a_hist = [a[PROMPT_TILE - SUBLANES:] for a in a_parts]
        yp_ref[0] = y_out
        for j in range(N_FF_CHUNKS):
            carry_a[:, j * FF_CHUNK:(j + 1) * FF_CHUNK] = a_hist[j]
            nconvp_ref[0, 0, :, j * FF_CHUNK:(j + 1) * FF_CHUNK] = a_hist[j][SUBLANES - (CONV_K - 1):]

    @pl.when(step >= n_prompt_steps)
    def _sample():
        rt = SAMPLE_SUBTILE
        nb = rt // dec_seq

        def mixer(s):
            rows, bat = slice(s * rt, (s + 1) * rt), slice(s * nb, (s + 1) * nb)
            x = xs_ref[rows]
            h = _rms(x, g1_ref[...]).astype(BF16)
            p, u, v, ta, tb = _input_proj(h, w_in_ref, sgug_ref)
            vs_ref[rows] = v

            stp = stp_ref[:, bat, :]
            old = jnp.concatenate([stp.reshape(POOL_PAD * nb, D_POOL), jnp.zeros((nb, D_POOL), F32)],
                                  axis=0)
            p_hi, p_lo = _split_bf16(p)
            s_hi, s_lo = _split_bf16(old)
            means = []
            for g in range(N_GROUPS):
                sl = slice(g * D_GROUP, (g + 1) * D_GROUP)
                new2 = jnp.concatenate([p_hi[:, sl], p_lo[:, sl]], axis=1)
                old2 = jnp.concatenate([s_hi[:, sl], s_lo[:, sl]], axis=1)
                r = _dot(pool_a_ref[g], new2) + _dot(pool_b_ref[g], old2)
                means.append(r[:, :D_GROUP] + r[:, D_GROUP:])
            y = _pool_groups((jnp.concatenate(means, axis=1) - p).astype(BF16), pw_ref, pscale_ref)
            p3 = p.reshape(nb, dec_seq, D_POOL)
            for i in range(POOL_PAD):
                k = i + dec_seq
                npools_ref[i, bat, :] = stp[k] if k < POOL_PAD else p3[:, k - POOL_PAD, :]

            vb = v.astype(BF16)
            mixes = []
            for hh in range(N_HEADS):
                bias = jnp.tile(bias_ref[0:dec_seq, hh * D_HEAD:(hh + 1) * D_HEAD], (nb, 1))
                mixes.append(_dot(wsb_ref[hh], vb[:, hh * D_HEAD:(hh + 1) * D_HEAD]) + bias)
            mix = jnp.concatenate(mixes, axis=1)
            return _mixer_merge(x, ta, tb, y, mix, u, wpo_ref, wso_ref, wo_ref)

        n_sub = SAMPLE_TILE // rt
        nb_all = n_sub * nb
        tok = lax.broadcasted_iota(jnp.int32, (nb_all, dec_seq, FF_CHUNK), 1)

        def shifted(j, a):
            lo, hi = j * FF_CHUNK, (j + 1) * FF_CHUNK
            a3 = a.reshape(nb_all, dec_seq, FF_CHUNK)
            back = stc_ref[0, :, :, lo:hi]
            back2, back1 = back[:, 0:1, :], back[:, 1:2, :]
            s1 = jnp.where(tok == 0, back1, pltpu.roll(a3, 1, axis=1))
            s2 = jnp.where(tok == 0, back2, jnp.where(tok == 1, back1, pltpu.roll(a3, 2, axis=1)))
            return s1.reshape(SAMPLE_TILE, FF_CHUNK), s2.reshape(SAMPLE_TILE, FF_CHUNK)

        x1s = [mixer(s) for s in range(n_sub)]
        y_out, a_parts = conv_ffn(jnp.concatenate(x1s, axis=0), shifted)
        ys_ref[...] = y_out
        for j, a in enumerate(a_parts):
            a3 = a.reshape(nb_all, dec_seq, FF_CHUNK)
            nconvs_ref[0, :, :, j * FF_CHUNK:(j + 1) * FF_CHUNK] = a3[:, dec_seq - (CONV_K - 1):, :]


def _pool_matrices(dec_seq, n_batch):
    a = np.zeros((N_GROUPS, n_batch * dec_seq, n_batch * dec_seq), np.float32)
    b = np.zeros((N_GROUPS, n_batch * dec_seq, POOL_HIST * n_batch), np.float32)
    for g, w in enumerate(POOL_WINDOWS):
        for bb in range(n_batch):
            for t in range(dec_seq):
                for k in range(w):
                    i = POOL_PAD + t - k
                    if i >= POOL_PAD:
                        a[g, bb * dec_seq + t, bb * dec_seq + i - POOL_PAD] = 1.0 / w
                    else:
                        b[g, bb * dec_seq + t, i * n_batch + bb] = 1.0 / w
    return jnp.asarray(a, BF16), jnp.asarray(b, BF16)


def _first_rows_table():
    t = np.arange(POOL_HIST, dtype=np.float32)[:, None]
    w = np.repeat(np.asarray(POOL_WINDOWS, np.float32), D_GROUP)[None, :]
    return jnp.asarray(1.0 / np.minimum(w, t + 1.0), F32), jnp.asarray(1.0 / w, F32)


def _whole(shape):
    zeros = (0,) * len(shape)
    return pl.BlockSpec(shape, lambda i: zeros, pipeline_mode=pl.Buffered(1))


def kernel(x_prompt, x_sample, state_pool, state_ffn_conv, norm1_g, w_in, pool_w, pool_scale,
           w_pool_out, sgu_norm_g, sgu_w, sgu_b, w_sgu_out, w_o, norm2_g, ffn_w_up, ffn_w_gate,
           ffn_conv_w, ffn_conv_b, ffn_w_down, final_norm_g):
    depth = norm1_g.shape[0]
    assert depth == 1
    batch, seq, _ = x_prompt.shape
    dec_batch, dec_seq, _ = x_sample.shape
    assert seq % PROMPT_TILE == 0 and PROMPT_SUBTILE % CHUNK == 0 and PROMPT_SUBTILE >= 2 * POOL_HIST
    assert SAMPLE_SUBTILE % dec_seq == 0 and (dec_batch * dec_seq) % SAMPLE_TILE == 0
    assert CONV_K - 1 <= dec_seq <= CHUNK and dec_seq % SUBLANES == 0
    assert sgu_w.shape[-1] == CHUNK and state_pool.shape[2] == POOL_PAD

    tiles_per_seq = seq // PROMPT_TILE
    n_p = batch * tiles_per_seq
    rows = dec_batch * dec_seq
    n_s = rows // SAMPLE_TILE
    tile_batch = SAMPLE_TILE // dec_seq

    tbl, invw = _first_rows_table()
    pool_a, pool_b = _pool_matrices(dec_seq, SAMPLE_SUBTILE // dec_seq)
    onehot = np.zeros((CHUNK, SAMPLE_SUBTILE), np.float32)
    onehot[np.arange(SAMPLE_SUBTILE) % dec_seq, np.arange(SAMPLE_SUBTILE)] = 1.0
    onehot = jnp.asarray(onehot, BF16)
    wscale = jnp.asarray(np.where(np.arange(D_IN) < D_POOL, 1.0, 0.5)[None, :], F32)

    p_tile = lambda i: jnp.minimum(i, n_p - 1)
    s_tile = lambda i: jnp.maximum(i - n_p, 0)
    s_rows = lambda n: pl.BlockSpec((SAMPLE_TILE, n), lambda i: (s_tile(i), 0), pipeline_mode=pl.Buffered(1))
    hbm = pl.BlockSpec(memory_space=pl.ANY)

    s_pool = pl.BlockSpec((POOL_PAD, tile_batch, D_POOL), lambda i: (0, s_tile(i), 0),
                          pipeline_mode=pl.Buffered(1))
    s_conv = pl.BlockSpec((1, tile_batch, CONV_K - 1, D_FF), lambda i: (0, s_tile(i), 0, 0),
                          pipeline_mode=pl.Buffered(1))
    tiled = [x_prompt, x_sample.reshape(rows, D_MODEL), jnp.transpose(state_pool[0], (1, 0, 2)),
             state_ffn_conv]
    tiled_specs = [
        pl.BlockSpec((1, PROMPT_TILE, D_MODEL), lambda i: (p_tile(i) // tiles_per_seq, p_tile(i) % tiles_per_seq, 0)),
        s_rows(D_MODEL), s_pool, s_conv]
    small = [norm1_g, pool_scale, sgu_norm_g, sgu_b, norm2_g, jnp.transpose(ffn_conv_w, (1, 0, 2)), ffn_conv_b,
             final_norm_g.reshape(1, D_MODEL), sgu_w, pool_w, invw, tbl, pool_a, pool_b, onehot, wscale]
    big = [w_in, w_pool_out, w_sgu_out, w_o, ffn_w_up, ffn_w_gate, ffn_w_down]

    out_shape = [
        jax.ShapeDtypeStruct((batch, seq, D_MODEL), F32),
        jax.ShapeDtypeStruct((rows, D_MODEL), F32),
        jax.ShapeDtypeStruct((POOL_PAD, batch, D_POOL), F32),
        jax.ShapeDtypeStruct((1, batch, CONV_K - 1, D_FF), F32),
        jax.ShapeDtypeStruct((POOL_PAD, dec_batch, D_POOL), F32),
        jax.ShapeDtypeStruct((1, dec_batch, CONV_K - 1, D_FF), F32),
        jax.ShapeDtypeStruct((rows, D_SGU), F32)]
    out_specs = [
        pl.BlockSpec((1, PROMPT_TILE, D_MODEL), lambda i: (p_tile(i) // tiles_per_seq, p_tile(i) % tiles_per_seq, 0)),
        s_rows(D_MODEL),
        pl.BlockSpec((POOL_PAD, batch, D_POOL), lambda i: (0, 0, 0)),
        pl.BlockSpec((1, 1, CONV_K - 1, D_FF), lambda i: (0, p_tile(i) // tiles_per_seq, 0, 0)),
        s_pool, s_conv, s_rows(D_SGU)]
    scratch = [
        pltpu.VMEM((D_MODEL, D_IN), BF16), pltpu.VMEM((D_POOL, D_MODEL), BF16), pltpu.VMEM((D_SGU, D_MODEL), BF16),
        pltpu.VMEM((D_MODEL, D_MODEL), BF16), pltpu.VMEM((D_MODEL, D_FF), BF16), pltpu.VMEM((D_MODEL, D_FF), BF16),
        pltpu.VMEM((D_FF, D_MODEL), BF16),
        pltpu.VMEM((N_GROUPS // 2, 2 * D_GROUP, 2 * D_GROUP), BF16),
        pltpu.VMEM((N_HEADS, CHUNK, CHUNK), BF16),
        pltpu.VMEM((CHUNK, D_SGU), F32),
        pltpu.VMEM((POOL_HIST, D_POOL), F32), pltpu.VMEM((SUBLANES, D_FF), F32),
        pltpu.VMEM((CONV_K, 1, D_FF), F32), pltpu.VMEM((1, D_FF), F32),
        pltpu.VMEM((N_HEADS, SAMPLE_SUBTILE, SAMPLE_SUBTILE), BF16)]

    outs = pl.pallas_call(
        functools.partial(_layer_kernel, n_prompt_steps=n_p, tiles_per_seq=tiles_per_seq, dec_seq=dec_seq),
        grid=(n_p + n_s,),
        in_specs=tiled_specs + [_whole(a.shape) for a in small] + [hbm] * len(big),
        out_specs=out_specs,
        out_shape=out_shape,
        scratch_shapes=scratch,
        compiler_params=pltpu.CompilerParams(dimension_semantics=("arbitrary",),
                                             vmem_limit_bytes=VMEM_LIMIT_BYTES),
        name="layer_step",
    )(*tiled, *small, *big)
    y_prompt, y_s, npool_p, nconv_p, npool_s, nconv_s, v_s = outs
    return (y_prompt, y_s.reshape(dec_batch, dec_seq, D_MODEL),
            jnp.transpose(npool_p, (1, 0, 2))[None], jnp.transpose(npool_s, (1, 0, 2))[None],
            nconv_p, nconv_s,
            v_s.reshape(1, dec_batch, dec_seq, D_SGU))
```
